```python
import math
import jax, jax.numpy as jnp
from jax import lax
import numpy as np

D_MODEL = 1024
BATCH = 8
SEQ = 4096
DEPTH = 1

MLA_HEADS = 8
MLA_NOPE_DIM = 64
MLA_ROPE_DIM = 32
MLA_V_DIM = 64
MLA_Q_RANK = 384
MLA_KV_RANK = 128
MLA_WIDTH = MLA_HEADS * MLA_V_DIM
MLA_QK_DIM = MLA_NOPE_DIM + MLA_ROPE_DIM
ROPE_THETA = 10000.0
Q_BLOCK = 128
HGRN_HEADS = 4
HGRN_HEAD_DIM = 128
HGRN_WIDTH = HGRN_HEADS * HGRN_HEAD_DIM
HGRN_CHUNK = 64
D_MIX = MLA_WIDTH + HGRN_WIDTH
D_IN_PROJ = MLA_Q_RANK + MLA_KV_RANK + MLA_ROPE_DIM + 4 * HGRN_WIDTH
D_FF = -(-8 * D_MODEL // (3 * 256)) * 256
EPS = 1e-6

kernel_name = 'hymba_mla_hgrn2_sandwich_block'


def rmsnorm(x, w):
    x32 = x.astype(jnp.float32)
    inv = lax.rsqrt(jnp.mean(x32 * x32, axis=-1, keepdims=True) + EPS)
    return (x32 * inv).astype(x.dtype) * w


def rope_cos_sin(positions, dtype):
    inv_freq = 1.0 / (ROPE_THETA ** (jnp.arange(0, MLA_ROPE_DIM, 2, dtype=jnp.float32) / MLA_ROPE_DIM))
    ang = positions.astype(jnp.float32)[..., None] * inv_freq
    return jnp.cos(ang).astype(dtype), jnp.sin(ang).astype(dtype)


def apply_rope(x, cos, sin):
    x1, x2 = jnp.split(x, 2, axis=-1)
    return jnp.concatenate([x1 * cos - x2 * sin, x1 * sin + x2 * cos], axis=-1)


def mla_mixer(c_q, c_kv, k_rope_raw, positions, q_norm_w, w_uq, kv_norm_w, w_ukv, out_norm_w):
    B, S, _ = c_q.shape
    nb = S // Q_BLOCK
    q = jnp.einsum('bsr,rhd->bshd', rmsnorm(c_q, q_norm_w), w_uq)
    q_nope, q_rope = q[..., :MLA_NOPE_DIM], q[..., MLA_NOPE_DIM:]
    kv = jnp.einsum('bsr,rhd->bshd', rmsnorm(c_kv, kv_norm_w), w_ukv)
    k_nope, v = kv[..., :MLA_NOPE_DIM], kv[..., MLA_NOPE_DIM:]
    cos, sin = rope_cos_sin(positions, q.dtype)
    q_rope = apply_rope(q_rope, cos[:, :, None, :], sin[:, :, None, :])
    k_rope = apply_rope(k_rope_raw, cos, sin)
    scale = MLA_QK_DIM ** -0.5
    qn_b = q_nope.reshape(B, nb, Q_BLOCK, MLA_HEADS, MLA_NOPE_DIM).transpose(1, 0, 2, 3, 4)
    qr_b = q_rope.reshape(B, nb, Q_BLOCK, MLA_HEADS, MLA_ROPE_DIM).transpose(1, 0, 2, 3, 4)
    key_idx = jnp.arange(S)

    def block(args):
        qn, qr, blk = args
        s = (jnp.einsum('bqhd,bkhd->bhqk', qn, k_nope)
             + jnp.einsum('bqhd,bkd->bhqk', qr, k_rope)).astype(jnp.float32) * scale
        q_idx = blk * Q_BLOCK + jnp.arange(Q_BLOCK)
        mask = key_idx[None, :] <= q_idx[:, None]
        p = jax.nn.softmax(jnp.where(mask, s, -jnp.inf), axis=-1).astype(v.dtype)
        return jnp.einsum('bhqk,bkhd->bqhd', p, v)

    o = lax.map(block, (qn_b, qr_b, jnp.arange(nb)))
    o = o.transpose(1, 0, 2, 3, 4).reshape(B, S, MLA_HEADS, MLA_V_DIM)
    o = rmsnorm(o, out_norm_w.reshape(MLA_HEADS, MLA_V_DIM))
    return o.reshape(B, S, MLA_WIDTH)


def hgrn2_mixer(q_raw, f_raw, i_raw, g_raw, lb, out_norm_w):
    B, S, _ = q_raw.shape
    H, D, C = HGRN_HEADS, HGRN_HEAD_DIM, HGRN_CHUNK
    nc = S // C
    lb32 = lb.astype(jnp.float32)
    f = lb32 + (1.0 - lb32) * jax.nn.sigmoid(f_raw.astype(jnp.float32))
    log_f = jnp.log(f)
    k = 1.0 - f
    q = jax.nn.silu(q_raw.astype(jnp.float32))
    v = i_raw.astype(jnp.float32)

    def to_chunks(t):
        return t.reshape(B, nc, C, H, D).transpose(1, 0, 3, 2, 4)

    causal = jnp.tril(jnp.ones((C, C), dtype=bool))[:, :, None]

    def step(state, inp):
        qc, kc, vc, lfc = inp
        b = jnp.cumsum(lfc, axis=2)
        o_inter = jnp.einsum('bhtk,bhkv->bhtv', qc * jnp.exp(b), state)
        diff = b[:, :, :, None, :] - b[:, :, None, :, :]
        decay = jnp.exp(jnp.where(causal, diff, -jnp.inf))
        a = jnp.einsum('bhtk,bhtsk,bhsk->bhts', qc, decay, kc)
        o_intra = jnp.einsum('bhts,bhsv->bhtv', a, vc)
        b_last = b[:, :, -1:, :]
        k_dec = kc * jnp.exp(b_last - b)
        state = jnp.exp(b_last[:, :, 0, :])[..., None] * state + jnp.einsum('bhsk,bhsv->bhkv', k_dec, vc)
        return state, o_inter + o_intra

    s0 = jnp.zeros((B, H, D, D), jnp.float32)
    _, o = lax.scan(step, s0, (to_chunks(q), to_chunks(k), to_chunks(v), to_chunks(log_f)))
    o = o.transpose(1, 0, 3, 2, 4).reshape(B, S, H, D)
    o = rmsnorm(o, out_norm_w.reshape(H, D).astype(jnp.float32))
    o = o.reshape(B, S, HGRN_WIDTH) * jax.nn.silu(g_raw.astype(jnp.float32))
    return o.astype(q_raw.dtype)


def setup_inputs(seed: int = 0) -> dict:
    key = jax.random.key(seed)
    ks = jax.random.split(key, 24)
    nrm = lambda k, shape, fan_in: jax.random.normal(k, shape, jnp.float32) * fan_in ** -0.5
    gain = lambda k, shape: 1.0 + 0.02 * jax.random.normal(k, shape, jnp.float32)
    x = jax.random.normal(ks[0], (BATCH, SEQ, D_MODEL), jnp.float32)
    offset = jax.random.randint(ks[1], (BATCH, 1), 0, 2048, dtype=jnp.int32)
    positions = (offset + jnp.arange(SEQ, dtype=jnp.int32)[None, :]).astype(jnp.int32)
    lb_base = jnp.concatenate([-jnp.ones((1, HGRN_WIDTH), jnp.float32),
                               jnp.ones((DEPTH, HGRN_WIDTH), jnp.float32)], axis=0)
    hgrn_lb_logits = lb_base + 0.1 * jax.random.normal(ks[2], (DEPTH + 1, HGRN_WIDTH), jnp.float32)
    return {
        'x': x,
        'positions': positions,
        'attn_pre_norm': gain(ks[3], (DEPTH, D_MODEL)),
        'w_in': nrm(ks[4], (DEPTH, D_MODEL, D_IN_PROJ), D_MODEL),
        'mla_q_norm': gain(ks[5], (DEPTH, MLA_Q_RANK)),
        'mla_w_uq': nrm(ks[6], (DEPTH, MLA_Q_RANK, MLA_HEADS, MLA_QK_DIM), MLA_Q_RANK),
        'mla_kv_norm': gain(ks[7], (DEPTH, MLA_KV_RANK)),
        'mla_w_ukv': nrm(ks[8], (DEPTH, MLA_KV_RANK, MLA_HEADS, MLA_NOPE_DIM + MLA_V_DIM), MLA_KV_RANK),
        'mla_out_norm': gain(ks[9], (DEPTH, MLA_WIDTH)),
        'hgrn_lb_logits': hgrn_lb_logits,
        'hgrn_out_norm': gain(ks[10], (DEPTH, HGRN_WIDTH)),
        'w_out': nrm(ks[11], (DEPTH, D_MIX, D_MODEL), D_MIX),
        'attn_post_norm': gain(ks[12], (DEPTH, D_MODEL)),
        'ffn_pre_norm': gain(ks[13], (DEPTH, D_MODEL)),
        'w_gate': nrm(ks[14], (DEPTH, D_MODEL, D_FF), D_MODEL),
        'w_up': nrm(ks[15], (DEPTH, D_MODEL, D_FF), D_MODEL),
        'w_down': nrm(ks[16], (DEPTH, D_FF, D_MODEL), D_FF),
        'ffn_post_norm': gain(ks[17], (DEPTH, D_MODEL)),
    }


def reference(x, positions, attn_pre_norm, w_in, mla_q_norm, mla_w_uq, mla_kv_norm, mla_w_ukv,
              mla_out_norm, hgrn_lb_logits, hgrn_out_norm, w_out, attn_post_norm, ffn_pre_norm,
              w_gate, w_up, w_down, ffn_post_norm):
    lb_all = jnp.cumsum(jax.nn.softmax(hgrn_lb_logits.astype(jnp.float32), axis=0), axis=0)[:DEPTH]
    s1 = MLA_Q_RANK
    s2 = s1 + MLA_KV_RANK
    s3 = s2 + MLA_ROPE_DIM
    s4 = s3 + HGRN_WIDTH
    s5 = s4 + HGRN_WIDTH
    s6 = s5 + HGRN_WIDTH
    h = x
    for l in range(DEPTH):
        u = rmsnorm(h, attn_pre_norm[l])
        xp = jnp.einsum('bsd,de->bse', u, w_in[l])
        c_q, c_kv, k_rope_raw = xp[..., :s1], xp[..., s1:s2], xp[..., s2:s3]
        hq, hf, hi, hg = xp[..., s3:s4], xp[..., s4:s5], xp[..., s5:s6], xp[..., s6:]
        o_mla = mla_mixer(c_q, c_kv, k_rope_raw, positions, mla_q_norm[l], mla_w_uq[l],
                          mla_kv_norm[l], mla_w_ukv[l], mla_out_norm[l])
        o_hgrn = hgrn2_mixer(hq, hf, hi, hg, lb_all[l], hgrn_out_norm[l])
        mix = jnp.concatenate([o_mla, o_hgrn.astype(o_mla.dtype)], axis=-1)
        h = h + rmsnorm(jnp.einsum('bse,ed->bsd', mix, w_out[l]), attn_post_norm[l])
        z = rmsnorm(h, ffn_pre_norm[l])
        ff = jax.nn.silu(jnp.einsum('bsd,df->bsf', z, w_gate[l])) * jnp.einsum('bsd,df->bsf', z, w_up[l])
        h = h + rmsnorm(jnp.einsum('bsf,fd->bsd', ff, w_down[l]), ffn_post_norm[l])
    return h
```

```python
import functools

import numpy as np
import jax
import jax.numpy as jnp
from jax import lax
from jax.experimental import pallas as pl
from jax.experimental.pallas import tpu as pltpu

F32 = jnp.float32
BF16 = jnp.bfloat16

EPS = 1e-6
ROPE_THETA = 10000.0
LANES = 128
MLA_NOPE = 64
MLA_ROPE = 32
MLA_V = 64
HGRN_HEAD = 128
HGRN_CHUNK = 64
NEG_BIG = -1e30
MIB = 1024 * 1024


def _rms(x, w):
    inv = lax.rsqrt(jnp.mean(x * x, axis=-1, keepdims=True) + EPS)
    return (x * inv) * w


def _sigmoid(x):
    return 1.0 / (1.0 + jnp.exp(-x))


def _dot(a, b):
    return jnp.dot(a, b, preferred_element_type=F32)


def _dot_nt(a, b):
    return lax.dot_general(a, b, (((1,), (1,)), ((), ())), preferred_element_type=F32)


def _dot_tn(a, b):
    return lax.dot_general(a, b, (((0,), (0,)), ((), ())), preferred_element_type=F32)


def _rope_kernel(pos_ref, invf_ref, cos_ref, sin_ref, nsin_ref):
    ang = pos_ref[...].astype(F32) * invf_ref[...]
    c = jnp.cos(ang)
    s = jnp.sin(ang)
    cos_ref[...] = c
    sin_ref[...] = s
    nsin_ref[...] = -s


def _rope_tables(positions):
    n_tok = positions.size
    half = MLA_ROPE // 2
    inv_freq = 1.0 / (ROPE_THETA ** (jnp.arange(0, MLA_ROPE, 2, dtype=F32) / MLA_ROPE))
    per_row = LANES // half
    rows = n_tok // per_row
    pos_rep = jnp.repeat(positions.reshape(-1), half).reshape(rows, LANES)
    invf = jnp.tile(inv_freq, per_row).reshape(1, LANES)
    blk = min(rows, 512)
    spec = pl.BlockSpec((blk, LANES), lambda i: (i, 0))
    shp = jax.ShapeDtypeStruct((rows, LANES), F32)
    cos, sin, nsin = pl.pallas_call(
        _rope_kernel,
        grid=(rows // blk,),
        in_specs=[spec, pl.BlockSpec((1, LANES), lambda i: (0, 0))],
        out_specs=[spec, spec, spec],
        out_shape=[shp, shp, shp],
        name="rope_tables",
    )(pos_rep, invf)
    cos = cos.reshape(n_tok, half)
    sin = sin.reshape(n_tok, half)
    nsin = nsin.reshape(n_tok, half)
    z_lo = jnp.zeros((n_tok, MLA_NOPE), F32)
    z_hi = jnp.zeros((n_tok, LANES - MLA_NOPE - MLA_ROPE), F32)
    ck = jnp.concatenate([z_lo, cos, cos, z_hi], axis=-1)
    sk = jnp.concatenate([z_lo, nsin, sin, z_hi], axis=-1)
    return ck, sk


def _inproj_kernel(x_ref, ck_ref, sk_ref, prew_ref, win_ref, qnw_ref, wq_ref, kvnw_ref, wkv_ref,
                   lbl_ref,
                   q_out, k_out, v_out, hq_out, lf_out, kk_out, hi_out, g_out,
                   *, scale, n_heads, q_rank, kv_rank, hw):
    x = x_ref[0]
    u = _rms(x, prew_ref[...]).astype(BF16)
    ck = ck_ref[...]
    sk = sk_ref[...]
    lane = lax.broadcasted_iota(jnp.int32, ck.shape, 1)
    cq = (ck + jnp.where(lane < MLA_NOPE, 1.0, 0.0)) * scale
    sq = sk * scale

    c0 = 0
    c_q = _dot(u, win_ref[:, c0:c0 + q_rank])
    c0 += q_rank
    c_kv = _dot(u, win_ref[:, c0:c0 + kv_rank])
    c0 += kv_rank
    kr = _dot(u, win_ref[:, c0:c0 + LANES])
    c0 += LANES
    krs = _dot(u, win_ref[:, c0:c0 + LANES])
    c0 += LANES

    hp = n_heads * LANES
    cqn = _rms(c_q, qnw_ref[...]).astype(BF16)
    qq = _dot(cqn, wq_ref[...])
    for h in range(n_heads):
        qh = qq[:, h * LANES:(h + 1) * LANES] * cq + qq[:, hp + h * LANES:hp + (h + 1) * LANES] * sq
        q_out[0, h] = qh.astype(BF16)

    ckvn = _rms(c_kv, kvnw_ref[...]).astype(BF16)
    kvv = _dot(ckvn, wkv_ref[...])
    k_rope = kr * ck + krs * sk
    for h in range(n_heads):
        k_out[0, h] = (kvv[:, h * LANES:(h + 1) * LANES] + k_rope).astype(BF16)
    v_out[0] = kvv[:, hp:].astype(BF16)

    lbl = lbl_ref[...]
    e = jnp.exp(lbl - jnp.max(lbl, axis=0, keepdims=True))
    lb = e[0:1] / jnp.sum(e, axis=0, keepdims=True)

    hq = _dot(u, win_ref[:, c0:c0 + hw])
    c0 += hw
    hq_out[0] = (hq * _sigmoid(hq)).astype(BF16)
    hf = _dot(u, win_ref[:, c0:c0 + hw])
    c0 += hw
    f = lb + (1.0 - lb) * _sigmoid(hf)
    lf_out[0] = jnp.log(f)
    kk_out[0] = (1.0 - f).astype(BF16)
    hi = _dot(u, win_ref[:, c0:c0 + hw])
    c0 += hw
    hi_out[0] = hi.astype(BF16)
    hg = _dot(u, win_ref[:, c0:c0 + hw])
    g_out[0] = (hg * _sigmoid(hg)).astype(BF16)


def _in_proj(x, ck, sk, prew, win_ext, qnw, wq_cat, kvnw, wkv_cat, lbl, *, n_heads, hw, tm):
    b, s, d = x.shape
    q_rank = qnw.shape[-1]
    kv_rank = kvnw.shape[-1]
    ns = s // tm
    scale = (MLA_NOPE + MLA_ROPE) ** -0.5
    kern = functools.partial(_inproj_kernel, scale=scale, n_heads=n_heads, q_rank=q_rank,
                             kv_rank=kv_rank, hw=hw)

    def full(a):
        return pl.BlockSpec(a.shape, lambda i, j: (0,) * a.ndim)

    tok = lambda w: pl.BlockSpec((1, tm, w), lambda i, j: (i, j, 0))
    head = pl.BlockSpec((1, n_heads, tm, LANES), lambda i, j: (i, 0, j, 0))
    tab = pl.BlockSpec((tm, LANES), lambda i, j: (i * ns + j, 0))
    vw = n_heads * MLA_V
    out_shape = [
        jax.ShapeDtypeStruct((b, n_heads, s, LANES), BF16),
        jax.ShapeDtypeStruct((b, n_heads, s, LANES), BF16),
        jax.ShapeDtypeStruct((b, s, vw), BF16),
        jax.ShapeDtypeStruct((b, s, hw), BF16),
        jax.ShapeDtypeStruct((b, s, hw), F32),
        jax.ShapeDtypeStruct((b, s, hw), BF16),
        jax.ShapeDtypeStruct((b, s, hw), BF16),
        jax.ShapeDtypeStruct((b, s, hw), BF16),
    ]
    return pl.pallas_call(
        kern,
        grid=(b, ns),
        in_specs=[tok(d), tab, tab, full(prew), full(win_ext), full(qnw), full(wq_cat), full(kvnw),
                  full(wkv_cat), full(lbl)],
        out_specs=[head, head, tok(vw), tok(hw), tok(hw), tok(hw), tok(hw), tok(hw)],
        out_shape=out_shape,
        compiler_params=pltpu.CompilerParams(
            dimension_semantics=("parallel", "parallel"), vmem_limit_bytes=56 * MIB),
        name="in_proj",
    )(x, ck, sk, prew, win_ext, qnw, wq_cat, kvnw, wkv_cat, lbl)


def _attn_kernel(q_ref, k_ref, v_ref, w_ref, o_ref, *, tq):
    qi = pl.program_id(2)
    q0 = q_ref[0, 0]
    q1 = q_ref[0, 1]

    def chunk(carry, start, masked):
        vc = v_ref[0, pl.ds(start, tq), :]
        new = []
        for h, q in enumerate((q0, q1)):
            m, l, acc = carry[h]
            kc = k_ref[0, h, pl.ds(start, tq), :]
            s = _dot_nt(q, kc)
            if masked:
                row = lax.broadcasted_iota(jnp.int32, s.shape, 0)
                col = lax.broadcasted_iota(jnp.int32, s.shape, 1)
                s = jnp.where(col <= row, s, -jnp.inf)
            m_new = jnp.maximum(m, jnp.max(s, axis=-1, keepdims=True))
            alpha = jnp.exp(m - m_new)
            p = jnp.exp(s - m_new)
            l = alpha * l + jnp.sum(p, axis=-1, keepdims=True)
            acc = alpha * acc + _dot(p.astype(BF16), vc)
            new.append((m_new, l, acc))
        return tuple(new)

    init = tuple((jnp.full((tq, 1), NEG_BIG, F32), jnp.zeros((tq, 1), F32),
                  jnp.zeros((tq, LANES), F32)) for _ in range(2))
    carry = lax.fori_loop(
        0, qi, lambda j, c: chunk(c, pl.multiple_of(j * tq, tq), False), init)
    carry = chunk(carry, pl.multiple_of(qi * tq, tq), True)

    lane = lax.broadcasted_iota(jnp.int32, (tq, LANES), 1)
    first = lane < MLA_V
    o = jnp.where(first, carry[0][2] / carry[0][1], carry[1][2] / carry[1][1])
    sq = o * o
    ss_all = jnp.sum(sq, axis=-1, keepdims=True)
    ss0 = jnp.sum(jnp.where(first, sq, 0.0), axis=-1, keepdims=True)
    inv = lax.rsqrt(jnp.where(first, ss0, ss_all - ss0) * (1.0 / MLA_V) + EPS)
    o_ref[0] = ((o * inv) * w_ref[...]).astype(BF16)


def _mla_attention(q, k, v, w_norm, *, tq):
    b, n_heads, s, _ = q.shape
    kern = functools.partial(_attn_kernel, tq=tq)
    return pl.pallas_call(
        kern,
        grid=(b, n_heads // 2, s // tq),
        in_specs=[
            pl.BlockSpec((1, 2, tq, LANES), lambda i, j, t: (i, j, t, 0)),
            pl.BlockSpec((1, 2, s, LANES), lambda i, j, t: (i, j, 0, 0)),
            pl.BlockSpec((1, s, LANES), lambda i, j, t: (i, 0, j)),
            pl.BlockSpec((1, LANES), lambda i, j, t: (0, j)),
        ],
        out_specs=pl.BlockSpec((1, tq, LANES), lambda i, j, t: (i, t, j)),
        out_shape=jax.ShapeDtypeStruct((b, s, n_heads * MLA_V), BF16),
        compiler_params=pltpu.CompilerParams(
            dimension_semantics=("parallel", "parallel", "arbitrary"), vmem_limit_bytes=48 * MIB),
        name="mla_attn",
    )(q, k, v, w_norm)


def _hgrn_tables(c):
    n_lev = int(np.log2(c))
    mat = np.zeros((n_lev + 2, c, c), np.float32)
    lev = np.full((c, c), n_lev + 1, np.int32)
    for l in range(n_lev):
        m = c >> l
        half = m // 2
        for r in range(c):
            p = r % m
            mid = r - p + half
            if p >= half:
                mat[l, r, mid:r + 1] = 1.0
            else:
                mat[l, r, r + 1:mid] = 1.0
        for t in range(c):
            for s in range(c):
                if t // m == s // m and t % m >= half and s % m < half:
                    lev[t, s] = l
    for r in range(c):
        mat[n_lev, r, :r + 1] = 1.0
        mat[n_lev + 1, r, r + 1:] = 1.0
        lev[r, r] = n_lev
    mat = mat.reshape((n_lev + 2) * c, c)
    return np.concatenate([mat, mat, mat], axis=1), lev, n_lev


def _hgrn_kernel(q_ref, k_ref, v_ref, lf_ref, g_ref, w_ref, mat_ref, lev_ref, o_ref, st_ref,
                 *, n_chunks, c, n_lev):
    @pl.when(pl.program_id(2) == 0)
    def _():
        st_ref[...] = jnp.zeros_like(st_ref)

    mat = mat_ref[...]
    lev = lev_ref[...]
    w = w_ref[...]
    st = st_ref[...]
    for ci in range(n_chunks):
        rows = pl.ds(ci * c, c)
        q = q_ref[0, rows, :].astype(F32)
        k = k_ref[0, rows, :].astype(F32)
        v = v_ref[0, rows, :]
        lf = lf_ref[0, rows, :]
        hi = lf.astype(BF16)
        r1 = lf - hi.astype(F32)
        mid = r1.astype(BF16)
        lo = (r1 - mid.astype(F32)).astype(BF16)
        ex = jnp.exp(_dot(mat, jnp.concatenate([hi, mid, lo], axis=0)))

        a = jnp.zeros((c, c), F32)
        for l in range(n_lev):
            x = ex[l * c:(l + 1) * c]
            a = a + jnp.where(lev == l, _dot_nt((q * x).astype(BF16), (k * x).astype(BF16)), 0.0)
        a = a + jnp.where(lev == n_lev, _dot_nt(q.astype(BF16), k.astype(BF16)), 0.0)

        eb = ex[n_lev * c:(n_lev + 1) * c]
        ek = ex[(n_lev + 1) * c:(n_lev + 2) * c]
        o = _dot_nt((q * eb).astype(BF16), st.astype(BF16)) + _dot(a.astype(BF16), v)
        st = st * eb[c - 1:c, :] + _dot_tn(v, (k * ek).astype(BF16))

        o = _rms(o, w) * g_ref[0, rows, :].astype(F32)
        o_ref[0, rows, :] = o.astype(BF16)
    st_ref[...] = st


def _hgrn2(hq, kk, hi, lf, g, w_norm, *, tb):
    b, s, hw = hq.shape
    n_heads = hw // HGRN_HEAD
    c = HGRN_CHUNK
    mat_np, lev_np, n_lev = _hgrn_tables(c)
    mat = jnp.asarray(mat_np, BF16)
    lev = jnp.asarray(lev_np)
    kern = functools.partial(_hgrn_kernel, n_chunks=tb // c, c=c, n_lev=n_lev)
    tok = pl.BlockSpec((1, tb, HGRN_HEAD), lambda i, j, t: (i, t, j))
    return pl.pallas_call(
        kern,
        grid=(b, n_heads, s // tb),
        in_specs=[tok, tok, tok, tok, tok,
                  pl.BlockSpec((1, HGRN_HEAD), lambda i, j, t: (0, j)),
                  pl.BlockSpec(mat.shape, lambda i, j, t: (0, 0)),
                  pl.BlockSpec(lev.shape, lambda i, j, t: (0, 0))],
        out_specs=tok,
        out_shape=jax.ShapeDtypeStruct((b, s, hw), BF16),
        scratch_shapes=[pltpu.VMEM((HGRN_HEAD, HGRN_HEAD), F32)],
        compiler_params=pltpu.CompilerParams(
            dimension_semantics=("parallel", "parallel", "arbitrary"), vmem_limit_bytes=32 * MIB),
        name="hgrn2",
    )(hq, kk, hi, lf, g, w_norm, mat, lev)


def _out_ffn_kernel(om_ref, oh_ref, x_ref, wo_ref, postw_ref, prew_ref, wgu_ref, wd_ref, fpostw_ref,
                    out_ref, *, d_ff):
    mix = jnp.concatenate([om_ref[...], oh_ref[...]], axis=-1)
    h = x_ref[...] + _rms(_dot(mix, wo_ref[...]), postw_ref[...])
    z = _rms(h, prew_ref[...]).astype(BF16)
    gu = _dot(z, wgu_ref[...])
    gate = gu[:, :d_ff]
    ff = (gate * _sigmoid(gate) * gu[:, d_ff:]).astype(BF16)
    out_ref[...] = h + _rms(_dot(ff, wd_ref[...]), fpostw_ref[...])


def _out_ffn(om, oh, x, wo, postw, prew, wgu, wd, fpostw, *, tm):
    n_tok, d = x.shape
    d_ff = wd.shape[0]
    kern = functools.partial(_out_ffn_kernel, d_ff=d_ff)

    def full(a):
        return pl.BlockSpec(a.shape, lambda i: (0,) * a.ndim, pipeline_mode=pl.Buffered(1))

    tok = lambda w: pl.BlockSpec((tm, w), lambda i: (i, 0))
    return pl.pallas_call(
        kern,
        grid=(n_tok // tm,),
        in_specs=[tok(om.shape[1]), tok(oh.shape[1]), tok(d), full(wo), full(postw), full(prew),
                  full(wgu), full(wd), full(fpostw)],
        out_specs=tok(d),
        out_shape=jax.ShapeDtypeStruct((n_tok, d), F32),
        compiler_params=pltpu.CompilerParams(
            dimension_semantics=("parallel",), vmem_limit_bytes=56 * MIB),
        name="out_ffn",
    )(om, oh, x, wo, postw, prew, wgu, wd, fpostw)


def _prep_in_proj_weights(w_in, w_uq, w_ukv, *, q_rank, kv_rank):
    d = w_in.shape[0]
    half = MLA_ROPE // 2
    s2 = q_rank + kv_rank
    s3 = s2 + MLA_ROPE
    kr = w_in[:, s2:s3]
    z_lo = jnp.zeros((d, MLA_NOPE), w_in.dtype)
    z_hi = jnp.zeros((d, LANES - MLA_NOPE - MLA_ROPE), w_in.dtype)
    kr_slab = jnp.concatenate([z_lo, kr, z_hi], axis=1)
    krs_slab = jnp.concatenate([z_lo, kr[:, half:], kr[:, :half], z_hi], axis=1)
    win_ext = jnp.concatenate([w_in[:, :s2], kr_slab, krs_slab, w_in[:, s3:]], axis=1).astype(BF16)

    n_heads = w_uq.shape[1]
    pad = LANES - MLA_NOPE - MLA_ROPE
    wq_pad = jnp.pad(w_uq, ((0, 0), (0, 0), (0, pad)))
    rope = w_uq[:, :, MLA_NOPE:]
    wq_sw = jnp.concatenate([jnp.zeros_like(w_uq[:, :, :MLA_NOPE]), rope[:, :, half:], rope[:, :, :half],
                             jnp.zeros_like(w_uq[:, :, :pad])], axis=2)
    wq_cat = jnp.concatenate([wq_pad.reshape(q_rank, n_heads * LANES),
                              wq_sw.reshape(q_rank, n_heads * LANES)], axis=1).astype(BF16)

    wk_pad = jnp.pad(w_ukv[:, :, :MLA_NOPE], ((0, 0), (0, 0), (0, LANES - MLA_NOPE)))
    wv = w_ukv[:, :, MLA_NOPE:]
    wkv_cat = jnp.concatenate([wk_pad.reshape(kv_rank, n_heads * LANES),
                               wv.reshape(kv_rank, n_heads * MLA_V)], axis=1).astype(BF16)
    return win_ext, wq_cat, wkv_cat


def kernel(x, positions, attn_pre_norm, w_in, mla_q_norm, mla_w_uq, mla_kv_norm, mla_w_ukv, mla_out_norm,
           hgrn_lb_logits, hgrn_out_norm, w_out, attn_post_norm, ffn_pre_norm, w_gate, w_up, w_down,
           ffn_post_norm):
    b, s, d = x.shape
    assert attn_pre_norm.shape[0] == 1, "single-layer block"
    q_rank = mla_q_norm.shape[-1]
    kv_rank = mla_kv_norm.shape[-1]
    n_heads = mla_w_uq.shape[2]
    hw = hgrn_out_norm.shape[-1]
    row = lambda a: a.reshape(1, -1)

    ck, sk = _rope_tables(positions)
    win_ext, wq_cat, wkv_cat = _prep_in_proj_weights(w_in[0], mla_w_uq[0], mla_w_ukv[0],
                                                     q_rank=q_rank, kv_rank=kv_rank)
    q, k, v, hq, lf, kk, hi, g = _in_proj(
        x, ck, sk, row(attn_pre_norm[0]), win_ext, row(mla_q_norm[0]), wq_cat, row(mla_kv_norm[0]),
        wkv_cat, hgrn_lb_logits, n_heads=n_heads, hw=hw, tm=512)
    o_mla = _mla_attention(q, k, v, row(mla_out_norm[0]), tq=512)
    o_hgrn = _hgrn2(hq, kk, hi, lf, g, row(hgrn_out_norm[0]), tb=512)

    wgu = jnp.concatenate([w_gate[0], w_up[0]], axis=1).astype(BF16)
    out = _out_ffn(o_mla.reshape(b * s, -1), o_hgrn.reshape(b * s, -1), x.reshape(b * s, d),
                   w_out[0].astype(BF16), row(attn_post_norm[0]), row(ffn_pre_norm[0]), wgu,
                   w_down[0].astype(BF16), row(ffn_post_norm[0]), tm=256)
    return out.reshape(b, s, d)
```

```python
import functools
import math

import numpy as np
import jax
import jax.numpy as jnp
from jax import lax
from jax.experimental import pallas as pl
from jax.experimental.pallas import tpu as pltpu

F32 = jnp.float32
BF16 = jnp.bfloat16

EPS = 1e-6
ROPE_THETA = 10000.0
LOG2E = math.log2(math.e)
LANES = 128
BF16_SUBLANES = 16
MLA_NOPE = 64
MLA_ROPE = 32
MLA_V = 64
ONES_ROWS = BF16_SUBLANES
F32_SUBLANES = 8
HGRN_HEAD = 128
HGRN_BLOCK = 256
HGRN_CUMSUM_GROUP = 64
NEG_BIG = -1e30
MIB = 1024 * 1024


def _rms(x, w):
    inv = lax.rsqrt(jnp.mean(x * x, axis=-1, keepdims=True) + EPS)
    return (x * inv) * w


def _sigmoid(x):
    return 1.0 / (1.0 + jnp.exp(-x))


def _dot(a, b):
    return jnp.dot(a, b, preferred_element_type=F32)


def _dot_nt(a, b):
    return lax.dot_general(a, b, (((1,), (1,)), ((), ())), preferred_element_type=F32)


def _dot_tn(a, b):
    return lax.dot_general(a, b, (((0,), (0,)), ((), ())), preferred_element_type=F32)


def _rope_kernel(pos_ref, invf_ref, cos_ref, sin_ref, nsin_ref):
    ang = pos_ref[...].astype(F32) * invf_ref[...]
    c = jnp.cos(ang)
    s = jnp.sin(ang)
    cos_ref[...] = c
    sin_ref[...] = s
    nsin_ref[...] = -s


def _rope_tables(positions):
    n_tok = positions.size
    half = MLA_ROPE // 2
    inv_freq = 1.0 / (ROPE_THETA ** (jnp.arange(0, MLA_ROPE, 2, dtype=F32) / MLA_ROPE))
    per_row = LANES // half
    rows = n_tok // per_row
    pos_rep = jnp.repeat(positions.reshape(-1), half).reshape(rows, LANES)
    invf = jnp.tile(inv_freq, per_row).reshape(1, LANES)
    blk = min(rows, 512)
    spec = pl.BlockSpec((blk, LANES), lambda i: (i, 0))
    shp = jax.ShapeDtypeStruct((rows, LANES), F32)
    cos, sin, nsin = pl.pallas_call(
        _rope_kernel,
        grid=(rows // blk,),
        in_specs=[spec, pl.BlockSpec((1, LANES), lambda i: (0, 0))],
        out_specs=[spec, spec, spec],
        out_shape=[shp, shp, shp],
        name="rope_tables",
    )(pos_rep, invf)
    cos = cos.reshape(n_tok, half)
    sin = sin.reshape(n_tok, half)
    nsin = nsin.reshape(n_tok, half)
    z_lo = jnp.zeros((n_tok, MLA_NOPE), F32)
    z_hi = jnp.zeros((n_tok, LANES - MLA_NOPE - MLA_ROPE), F32)
    ck = jnp.concatenate([z_lo, cos, cos, z_hi], axis=-1)
    sk = jnp.concatenate([z_lo, nsin, sin, z_hi], axis=-1)
    return ck, sk, cos.T, sin.T


def _inproj_kernel(x_ref, ck_ref, sk_ref, ct_ref, st_ref, prew_ref, win_ref, qnw_ref, wq_ref,
                   kvnw_ref, wkv_ref, lbl_ref,
                   qt_out, k_out, vt_out, hq_out, lf_out, kk_out, hi_out, g_out,
                   *, qscale, n_heads, q_rank, kv_rank, hw):
    x = x_ref[0]
    tm = x.shape[0]
    u = _rms(x, prew_ref[...]).astype(BF16)
    ck = ck_ref[...]
    sk = sk_ref[...]

    c0 = 0
    c_q = _dot(u, win_ref[:, c0:c0 + q_rank])
    c0 += q_rank
    c_kv = _dot(u, win_ref[:, c0:c0 + kv_rank])
    c0 += kv_rank
    kr = _dot(u, win_ref[:, c0:c0 + LANES])
    c0 += LANES
    krs = _dot(u, win_ref[:, c0:c0 + LANES])
    c0 += LANES

    cqn = _rms(c_q, qnw_ref[...]).astype(BF16)
    qq = _dot(cqn, wq_ref[...])
    ct = ct_ref[...] * qscale
    st = st_ref[...] * qscale
    cc = jnp.concatenate([ct, ct], axis=0)
    ss = jnp.concatenate([-st, st], axis=0)
    r0, r1 = MLA_NOPE, MLA_NOPE + MLA_ROPE
    zpad = jnp.zeros((LANES - r1, tm), F32)
    for h in range(n_heads):
        xt = qq[:, h * LANES:(h + 1) * LANES].T
        rope = xt[r0:r1] * cc + xt[r1:] * ss
        qt_out[0, h] = jnp.concatenate([xt[:r0] * qscale, rope, zpad], axis=0).astype(BF16)

    hp = n_heads * LANES
    ckvn = _rms(c_kv, kvnw_ref[...]).astype(BF16)
    kvv = _dot(ckvn, wkv_ref[...])
    k_rope = kr * ck + krs * sk
    for h in range(n_heads):
        k_out[0, h] = (kvv[:, h * LANES:(h + 1) * LANES] + k_rope).astype(BF16)
    vt_out[0] = kvv[:, hp:].T.astype(BF16)

    lbl = lbl_ref[...]
    e = jnp.exp(lbl - jnp.max(lbl, axis=0, keepdims=True))
    lb = e[0:1] / jnp.sum(e, axis=0, keepdims=True)

    hq = _dot(u, win_ref[:, c0:c0 + hw])
    c0 += hw
    hq_out[0] = (hq * _sigmoid(hq)).astype(BF16)
    hf = _dot(u, win_ref[:, c0:c0 + hw])
    c0 += hw
    f = lb + (1.0 - lb) * _sigmoid(hf)
    lf_out[0] = jnp.log(f) * LOG2E
    kk_out[0] = (1.0 - f).astype(BF16)
    hi = _dot(u, win_ref[:, c0:c0 + hw])
    c0 += hw
    hi_out[0] = hi.astype(BF16)
    hg = _dot(u, win_ref[:, c0:c0 + hw])
    g_out[0] = (hg * _sigmoid(hg)).astype(BF16)


def _in_proj(x, ck, sk, cos_t, sin_t, prew, win_ext, qnw, wq_pad, kvnw, wkv_cat, lbl, *, n_heads, hw, tm):
    b, s, d = x.shape
    q_rank = qnw.shape[-1]
    kv_rank = kvnw.shape[-1]
    ns = s // tm
    qscale = (MLA_NOPE + MLA_ROPE) ** -0.5 * LOG2E
    kern = functools.partial(_inproj_kernel, qscale=qscale, n_heads=n_heads, q_rank=q_rank,
                             kv_rank=kv_rank, hw=hw)

    def full(a):
        return pl.BlockSpec(a.shape, lambda i, j: (0,) * a.ndim)

    tok = lambda w: pl.BlockSpec((1, tm, w), lambda i, j: (i, j, 0))
    head = pl.BlockSpec((1, n_heads, tm, LANES), lambda i, j: (i, 0, j, 0))
    head_t = pl.BlockSpec((1, n_heads, LANES, tm), lambda i, j: (i, 0, 0, j))
    tab = pl.BlockSpec((tm, LANES), lambda i, j: (i * ns + j, 0))
    tab_t = pl.BlockSpec((cos_t.shape[0], tm), lambda i, j: (0, i * ns + j))
    vw = n_heads * MLA_V
    out_shape = [
        jax.ShapeDtypeStruct((b, n_heads, LANES, s), BF16),
        jax.ShapeDtypeStruct((b, n_heads, s, LANES), BF16),
        jax.ShapeDtypeStruct((b, vw, s), BF16),
        jax.ShapeDtypeStruct((b, s, hw), BF16),
        jax.ShapeDtypeStruct((b, s, hw), F32),
        jax.ShapeDtypeStruct((b, s, hw), BF16),
        jax.ShapeDtypeStruct((b, s, hw), BF16),
        jax.ShapeDtypeStruct((b, s, hw), BF16),
    ]
    return pl.pallas_call(
        kern,
        grid=(b, ns),
        in_specs=[tok(d), tab, tab, tab_t, tab_t, full(prew), full(win_ext), full(qnw), full(wq_pad),
                  full(kvnw), full(wkv_cat), full(lbl)],
        out_specs=[head_t, head, pl.BlockSpec((1, vw, tm), lambda i, j: (i, 0, j)),
                   tok(hw), tok(hw), tok(hw), tok(hw), tok(hw)],
        out_shape=out_shape,
        compiler_params=pltpu.CompilerParams(
            dimension_semantics=("parallel", "parallel"), vmem_limit_bytes=56 * MIB),
        name="in_proj",
    )(x, ck, sk, cos_t, sin_t, prew, win_ext, qnw, wq_pad, kvnw, wkv_cat, lbl)


def _attn_kernel(qt_ref, k_ref, vt_ref, w_ref, o_ref, sa_ref, sb_ref, *, tq, n_q):
    kc = tq // 2
    ones = jnp.ones((ONES_ROWS, kc), BF16)

    def q_tile(qi, _):
        q0 = pl.multiple_of(qi * tq, tq)

        def scores(h, ks, s_ref):
            s_t = _dot(k_ref[0, h, pl.ds(ks, kc), :], qt_ref[0, h, :, pl.ds(q0, tq)])
            s_ref[h] = s_t
            return jnp.max(s_t, axis=0, keepdims=True)

        def accumulate(h, ks, s_ref, cmax, m, acc, diag_offset=None):
            s_t = s_ref[h]
            if diag_offset is not None:
                key = lax.broadcasted_iota(jnp.int32, s_t.shape, 0) + diag_offset
                qry = lax.broadcasted_iota(jnp.int32, s_t.shape, 1)
                s_t = jnp.where(key <= qry, s_t, -jnp.inf)
                cmax = jnp.max(s_t, axis=0, keepdims=True)
            m_new = jnp.maximum(m, cmax)
            alpha = jnp.exp2(m - m_new)
            p_t = jnp.exp2(s_t - m_new).astype(BF16)
            v_ext = jnp.concatenate(
                [vt_ref[0, h * MLA_V:(h + 1) * MLA_V, pl.ds(ks, kc)], ones], axis=0)
            return m_new, alpha * acc + _dot(v_ext, p_t)

        def pair_body(t, carry):
            k0 = pl.multiple_of(t * tq, tq)
            new = []
            for h in range(2):
                cm_a, m, acc = carry[h]
                cm_b = scores(h, k0 + kc, sb_ref)
                m, acc = accumulate(h, k0, sa_ref, cm_a, m, acc)
                cm_a = scores(h, k0 + tq, sa_ref)
                m, acc = accumulate(h, k0 + kc, sb_ref, cm_b, m, acc)
                new.append((cm_a, m, acc))
            return tuple(new)

        init = tuple((scores(h, 0, sa_ref), jnp.full((1, tq), NEG_BIG, F32),
                      jnp.zeros((MLA_V + ONES_ROWS, tq), F32)) for h in range(2))
        carry = lax.fori_loop(0, qi, pair_body, init)

        outs = []
        for h in range(2):
            cm_a, m, acc = carry[h]
            scores(h, q0 + kc, sb_ref)
            m, acc = accumulate(h, q0, sa_ref, cm_a, m, acc, diag_offset=0)
            m, acc = accumulate(h, q0 + kc, sb_ref, None, m, acc, diag_offset=kc)
            o = acc[:MLA_V] / acc[MLA_V:MLA_V + 1]
            outs.append(o * lax.rsqrt(jnp.mean(o * o, axis=0, keepdims=True) + EPS))
        pair = jnp.concatenate(outs, axis=0).T
        o_ref[0, pl.ds(q0, tq), :] = (pair * w_ref[...]).astype(BF16)
        return 0

    lax.fori_loop(0, n_q, q_tile, 0)


def _mla_attention(qt, k, vt, w_norm, *, tq):
    b, n_heads, _, s = qt.shape
    kern = functools.partial(_attn_kernel, tq=tq, n_q=s // tq)
    s_scratch = pltpu.VMEM((2, tq // 2, tq), F32)
    return pl.pallas_call(
        kern,
        grid=(b, n_heads // 2),
        in_specs=[
            pl.BlockSpec((1, 2, LANES, s), lambda i, j: (i, j, 0, 0)),
            pl.BlockSpec((1, 2, s, LANES), lambda i, j: (i, j, 0, 0)),
            pl.BlockSpec((1, 2 * MLA_V, s), lambda i, j: (i, j, 0)),
            pl.BlockSpec((1, LANES), lambda i, j: (0, j)),
        ],
        out_specs=pl.BlockSpec((1, s, LANES), lambda i, j: (i, 0, j)),
        out_shape=jax.ShapeDtypeStruct((b, s, n_heads * MLA_V), BF16),
        scratch_shapes=[s_scratch, s_scratch],
        compiler_params=pltpu.CompilerParams(
            dimension_semantics=("parallel", "parallel"), vmem_limit_bytes=48 * MIB),
        name="mla_attn",
    )(qt, k, vt, w_norm)


def _hgrn_tables(c, g):
    n_lev = int(np.log2(c))
    t = np.arange(c)[:, None]
    s = np.arange(c)[None, :]
    lev = np.full((c, c), n_lev + 1, np.int32)
    for l in range(n_lev):
        m = c >> l
        half = m // 2
        lev[(t // m == s // m) & (t % m >= half) & (s % m < half)] = l
    lev[t == s] = n_lev
    hc = c // 2
    assert (lev[:hc, :hc] == lev[hc:, hc:]).all() and (lev[hc:, :hc] == 0).all()
    tri = np.tril(np.ones((g, g), np.float32))
    return np.concatenate([tri, tri, tri], axis=1), lev[:hc, :hc], n_lev


def _hgrn_level_exponents(b, lf, c):
    sub = lax.broadcasted_iota(jnp.int32, (F32_SUBLANES, LANES), 0)
    out = []
    m = c
    while m >= 2:
        half = m // 2
        if half >= F32_SUBLANES:
            pieces = []
            for lo in range(0, c, m):
                mid = lo + half
                r = b[mid - 1:mid]
                pieces += [r - b[lo:mid], b[mid:mid + half] - r]
            e = jnp.concatenate(pieces, axis=0)
        elif m == 2:
            odd = lax.broadcasted_iota(jnp.int32, lf.shape, 0) % 2 == 1
            e = jnp.where(odd, lf, 0.0)
        else:
            sign = jnp.where(sub % m >= half, 1.0, -1.0)
            pieces = []
            for v0 in range(0, c, F32_SUBLANES):
                ref = b[v0 + half - 1:v0 + half]
                for j in range(1, F32_SUBLANES // m):
                    ref = jnp.where(sub >= j * m, b[v0 + j * m + half - 1:v0 + j * m + half], ref)
                pieces.append(sign * (b[v0:v0 + F32_SUBLANES] - ref))
            e = jnp.concatenate(pieces, axis=0)
        out.append(e)
        m = half
    return out


def _hgrn_kernel(q_ref, k_ref, v_ref, lf_ref, g_ref, w_ref, tri_ref, lev_ref, o_ref,
                 *, c, g, n_blocks, n_hd, n_lev):
    tri = tri_ref[...]
    ng = c // g
    hc = c // 2

    def block(i, states):
        rows = pl.ds(pl.multiple_of(i * c, c), c)
        new_states = []
        for hd in range(n_hd):
            cols = slice(hd * HGRN_HEAD, (hd + 1) * HGRN_HEAD)
            st = states[hd]
            qb = q_ref[0, rows, cols]
            kb = k_ref[0, rows, cols]
            q = qb.astype(F32)
            k = kb.astype(F32)
            v = v_ref[0, rows, cols]
            lf = lf_ref[0, rows, cols]

            lf_w = jnp.concatenate([lf[j * g:(j + 1) * g] for j in range(ng)], axis=1)
            hi = lf_w.astype(BF16)
            r1 = lf_w - hi.astype(F32)
            mid = r1.astype(BF16)
            lo = (r1 - mid.astype(F32)).astype(BF16)
            b_w = _dot(tri, jnp.concatenate([hi, mid, lo], axis=0))
            parts = []
            off = None
            for j in range(ng):
                bj = b_w[:, j * LANES:(j + 1) * LANES]
                if off is not None:
                    bj = bj + off
                parts.append(bj)
                off = bj[g - 1:g]
            b = jnp.concatenate(parts, axis=0)

            lev = lev_ref[...]
            halves = (slice(0, hc), slice(hc, c))
            a_diag = [jnp.zeros((hc, hc), F32), jnp.zeros((hc, hc), F32)]
            a10 = None
            for l, e in enumerate(_hgrn_level_exponents(b, lf, c)):
                x = jnp.exp2(e)
                if l == 0:
                    a10 = _dot_nt((q[hc:] * x[hc:]).astype(BF16), (k[:hc] * x[:hc]).astype(BF16))
                    continue
                qt = (q * x).astype(BF16)
                kt = (k * x).astype(BF16)
                here = lev == l
                for i, rs in enumerate(halves):
                    a_diag[i] = jnp.where(here, _dot_nt(qt[rs], kt[rs]), a_diag[i])
            here = lev == n_lev
            for i, rs in enumerate(halves):
                a_diag[i] = jnp.where(here, _dot_nt(qb[rs], kb[rs]), a_diag[i])
            a = jnp.concatenate(
                [jnp.concatenate([a_diag[0], jnp.zeros((hc, hc), F32)], axis=1),
                 jnp.concatenate([a10, a_diag[1]], axis=1)], axis=0)

            eb = jnp.exp2(b)
            ek = jnp.exp2(off - b)
            o = _dot_nt((q * eb).astype(BF16), st.astype(BF16)) + _dot(a.astype(BF16), v)
            new_states.append(st * eb[c - 1:c] + _dot_tn(v, (k * ek).astype(BF16)))

            o = _rms(o, w_ref[:, cols]) * g_ref[0, rows, cols].astype(F32)
            o_ref[0, rows, cols] = o.astype(BF16)
        return tuple(new_states)

    init = tuple(jnp.zeros((HGRN_HEAD, HGRN_HEAD), F32) for _ in range(n_hd))
    lax.fori_loop(0, n_blocks // 2, lambda t, s: block(2 * t + 1, block(2 * t, s)), init)


def _hgrn2(hq, kk, hi, lf, g, w_norm, *, heads_per_step):
    b, s, hw = hq.shape
    c = HGRN_BLOCK
    width = heads_per_step * HGRN_HEAD
    tri_np, lev_np, n_lev = _hgrn_tables(c, HGRN_CUMSUM_GROUP)
    tri = jnp.asarray(tri_np, BF16)
    lev = jnp.asarray(lev_np)
    kern = functools.partial(_hgrn_kernel, c=c, g=HGRN_CUMSUM_GROUP, n_blocks=s // c,
                             n_hd=heads_per_step, n_lev=n_lev)
    tok = pl.BlockSpec((1, s, width), lambda i, j: (i, 0, j))
    return pl.pallas_call(
        kern,
        grid=(b, hw // width),
        in_specs=[tok, tok, tok, tok, tok,
                  pl.BlockSpec((1, width), lambda i, j: (0, j)),
                  pl.BlockSpec(tri.shape, lambda i, j: (0, 0)),
                  pl.BlockSpec(lev.shape, lambda i, j: (0, 0))],
        out_specs=tok,
        out_shape=jax.ShapeDtypeStruct((b, s, hw), BF16),
        compiler_params=pltpu.CompilerParams(
            dimension_semantics=("parallel", "parallel"), vmem_limit_bytes=48 * MIB),
        name="hgrn2",
    )(hq, kk, hi, lf, g, w_norm, tri, lev)


def _out_ffn_kernel(om_ref, oh_ref, x_ref, wo_ref, postw_ref, prew_ref, wgu_ref, wd_ref, fpostw_ref,
                    out_ref, *, d_ff):
    mix = jnp.concatenate([om_ref[...], oh_ref[...]], axis=-1)
    h = x_ref[...] + _rms(_dot(mix, wo_ref[...]), postw_ref[...])
    z = _rms(h, prew_ref[...]).astype(BF16)
    gu = _dot(z, wgu_ref[...])
    gate = gu[:, :d_ff]
    ff = (gate * _sigmoid(gate) * gu[:, d_ff:]).astype(BF16)
    out_ref[...] = h + _rms(_dot(ff, wd_ref[...]), fpostw_ref[...])


def _out_ffn(om, oh, x, wo, postw, prew, wgu, wd, fpostw, *, tm):
    n_tok, d = x.shape
    d_ff = wd.shape[0]
    kern = functools.partial(_out_ffn_kernel, d_ff=d_ff)

    def full(a):
        return pl.BlockSpec(a.shape, lambda i: (0,) * a.ndim, pipeline_mode=pl.Buffered(1))

    tok = lambda w: pl.BlockSpec((tm, w), lambda i: (i, 0))
    return pl.pallas_call(
        kern,
        grid=(n_tok // tm,),
        in_specs=[tok(om.shape[1]), tok(oh.shape[1]), tok(d), full(wo), full(postw), full(prew),
                  full(wgu), full(wd), full(fpostw)],
        out_specs=tok(d),
        out_shape=jax.ShapeDtypeStruct((n_tok, d), F32),
        compiler_params=pltpu.CompilerParams(
            dimension_semantics=("parallel",), vmem_limit_bytes=56 * MIB),
        name="out_ffn",
    )(om, oh, x, wo, postw, prew, wgu, wd, fpostw)


def _prep_in_proj_weights(w_in, w_uq, w_ukv, *, q_rank, kv_rank):
    d = w_in.shape[0]
    half = MLA_ROPE // 2
    s2 = q_rank + kv_rank
    s3 = s2 + MLA_ROPE
    kr = w_in[:, s2:s3]
    z_lo = jnp.zeros((d, MLA_NOPE), w_in.dtype)
    z_hi = jnp.zeros((d, LANES - MLA_NOPE - MLA_ROPE), w_in.dtype)
    kr_slab = jnp.concatenate([z_lo, kr, z_hi], axis=1)
    krs_slab = jnp.concatenate([z_lo, kr[:, half:], kr[:, :half], z_hi], axis=1)
    win_ext = jnp.concatenate([w_in[:, :s2], kr_slab, krs_slab, w_in[:, s3:]], axis=1).astype(BF16)

    n_heads = w_uq.shape[1]
    rope = w_uq[:, :, MLA_NOPE:]
    wq_pad = jnp.concatenate([w_uq, rope[:, :, half:], rope[:, :, :half]], axis=2)
    wq_pad = wq_pad.reshape(q_rank, n_heads * LANES).astype(BF16)

    wk_pad = jnp.pad(w_ukv[:, :, :MLA_NOPE], ((0, 0), (0, 0), (0, LANES - MLA_NOPE)))
    wv = w_ukv[:, :, MLA_NOPE:]
    wkv_cat = jnp.concatenate([wk_pad.reshape(kv_rank, n_heads * LANES),
                               wv.reshape(kv_rank, n_heads * MLA_V)], axis=1).astype(BF16)
    return win_ext, wq_pad, wkv_cat


def kernel(x, positions, attn_pre_norm, w_in, mla_q_norm, mla_w_uq, mla_kv_norm, mla_w_ukv, mla_out_norm,
           hgrn_lb_logits, hgrn_out_norm, w_out, attn_post_norm, ffn_pre_norm, w_gate, w_up, w_down,
           ffn_post_norm):
    b, s, d = x.shape
    assert attn_pre_norm.shape[0] == 1, "single-layer block"
    assert mla_w_uq.shape[3] == MLA_NOPE + MLA_ROPE and 2 * MLA_ROPE + MLA_NOPE == LANES
    q_rank = mla_q_norm.shape[-1]
    kv_rank = mla_kv_norm.shape[-1]
    n_heads = mla_w_uq.shape[2]
    hw = hgrn_out_norm.shape[-1]
    row = lambda a: a.reshape(1, -1)

    ck, sk, cos_t, sin_t = _rope_tables(positions)
    win_ext, wq_pad, wkv_cat = _prep_in_proj_weights(w_in[0], mla_w_uq[0], mla_w_ukv[0],
                                                     q_rank=q_rank, kv_rank=kv_rank)
    qt, k, vt, hq, lf, kk, hi, g = _in_proj(
        x, ck, sk, cos_t, sin_t, row(attn_pre_norm[0]), win_ext, row(mla_q_norm[0]), wq_pad,
        row(mla_kv_norm[0]), wkv_cat, hgrn_lb_logits, n_heads=n_heads, hw=hw, tm=512)
    o_mla = _mla_attention(qt, k, vt, row(mla_out_norm[0]), tq=512)
    o_hgrn = _hgrn2(hq, kk, hi, lf, g, row(hgrn_out_norm[0]), heads_per_step=2)

    wgu = jnp.concatenate([w_gate[0], w_up[0]], axis=1).astype(BF16)
    out = _out_ffn(o_mla.reshape(b * s, -1), o_hgrn.reshape(b * s, -1), x.reshape(b * s, d),
                   w_out[0].astype(BF16), row(attn_post_norm[0]), row(ffn_pre_norm[0]), wgu,
                   w_down[0].astype(BF16), row(ffn_post_norm[0]), tm=256)
    return out.reshape(b, s, d)
```

```python
import functools
import math

import numpy as np
import jax
import jax.numpy as jnp
from jax import lax
from jax.experimental import pallas as pl
from jax.experimental.pallas import tpu as pltpu

F32 = jnp.float32
BF16 = jnp.bfloat16

EPS = 1e-6
ROPE_THETA = 10000.0
LOG2E = math.log2(math.e)
LANES = 128
BF16_SUBLANES = 16
MLA_NOPE = 64
MLA_ROPE = 32
MLA_V = 64
ONES_ROWS = BF16_SUBLANES
F32_SUBLANES = 8
HGRN_HEAD = 128
HGRN_BLOCK = 256
HGRN_CUMSUM_GROUP = 64
BLOCKS_PER_TRIP = 4
NEG_BIG = -1e30
MIB = 1024 * 1024


def _rms(x, w):
    inv = lax.rsqrt(jnp.mean(x * x, axis=-1, keepdims=True) + EPS)
    return (x * inv) * w


def _sigmoid(x):
    return 1.0 / (1.0 + jnp.exp(-x))


def _dot(a, b):
    return jnp.dot(a, b, preferred_element_type=F32)


def _dot_nt(a, b):
    return lax.dot_general(a, b, (((1,), (1,)), ((), ())), preferred_element_type=F32)


def _dot_tn(a, b):
    return lax.dot_general(a, b, (((0,), (0,)), ((), ())), preferred_element_type=F32)


def _rope_kernel(pos_ref, invf_ref, cos_ref, sin_ref):
    ang = pos_ref[...].astype(F32) * invf_ref[...]
    cos_ref[...] = jnp.cos(ang)
    sin_ref[...] = jnp.sin(ang)


def _rope_tables(positions):
    n_tok = positions.size
    half = MLA_ROPE // 2
    inv_freq = 1.0 / (ROPE_THETA ** (jnp.arange(0, MLA_ROPE, 2, dtype=F32) / MLA_ROPE))
    blk = min(n_tok, 4096)
    spec = pl.BlockSpec((half, blk), lambda i: (0, i))
    shp = jax.ShapeDtypeStruct((half, n_tok), F32)
    return pl.pallas_call(
        _rope_kernel,
        grid=(n_tok // blk,),
        in_specs=[pl.BlockSpec((1, blk), lambda i: (0, i)), pl.BlockSpec((half, 1), lambda i: (0, 0))],
        out_specs=[spec, spec],
        out_shape=[shp, shp],
        name="rope_tables",
    )(positions.reshape(1, n_tok), inv_freq.reshape(half, 1))


def _inproj_kernel(x_ref, ct_ref, st_ref, prew_ref, win_ref, qnw_ref, wq_ref,
                   kvnw_ref, wkv_ref, lbl_ref,
                   qt_out, k_out, vt_out, hq_out, lf_out, kk_out, hi_out, g_out,
                   *, qscale, n_heads, q_rank, kv_rank, hw, n_sub):
    sub = x_ref.shape[1] // n_sub
    tiles = [slice(i * sub, (i + 1) * sub) for i in range(n_sub)]
    r0, r1 = MLA_NOPE, MLA_NOPE + MLA_ROPE
    hp = n_heads * LANES
    c0 = q_rank + kv_rank + LANES

    lbl = lbl_ref[...]
    e = jnp.exp(lbl - jnp.max(lbl, axis=0, keepdims=True))
    lb = e[0:1] / jnp.sum(e, axis=0, keepdims=True)

    us = [_rms(x_ref[0, rows, :], prew_ref[...]).astype(BF16) for rows in tiles]
    lows = [_dot(u, win_ref[:, :c0]) for u in us]

    def gate_group(g, u, rows):
        y = _dot(u, win_ref[:, c0 + g * hw:c0 + (g + 1) * hw])
        if g == 0:
            hq_out[0, rows, :] = (y * _sigmoid(y)).astype(BF16)
        elif g == 1:
            f = lb + (1.0 - lb) * _sigmoid(y)
            lf_out[0, rows, :] = jnp.log(f) * LOG2E
            kk_out[0, rows, :] = (1.0 - f).astype(BF16)
        elif g == 2:
            hi_out[0, rows, :] = y.astype(BF16)
        else:
            g_out[0, rows, :] = (y * _sigmoid(y)).astype(BF16)

    def mla_group(low, rows):
        c_q = low[:, :q_rank]
        c_kv = low[:, q_rank:q_rank + kv_rank]
        kr = low[:, q_rank + kv_rank:]
        ct = ct_ref[:, rows]
        st = st_ref[:, rows]
        cc = jnp.concatenate([ct, ct], axis=0)
        ss = jnp.concatenate([-st, st], axis=0)

        cqn = _rms(c_q, qnw_ref[...]).astype(BF16)
        qq = _dot(cqn, wq_ref[...])
        ccq = cc * qscale
        ssq = ss * qscale
        zpad = jnp.zeros((LANES - r1, sub), F32)
        for h in range(n_heads):
            xt = qq[:, h * LANES:(h + 1) * LANES].T
            rope = xt[r0:r1] * ccq + xt[r1:] * ssq
            qt_out[0, h, :, rows] = jnp.concatenate([xt[:r0] * qscale, rope, zpad], axis=0).astype(BF16)

        z64 = jnp.zeros((r0, sub), F32)
        ck = jnp.concatenate([z64, cc, zpad], axis=0).T
        sk = jnp.concatenate([z64, zpad, ss], axis=0).T
        k_rope = kr * ck + pltpu.roll(kr * sk, LANES - MLA_ROPE, 1)

        ckvn = _rms(c_kv, kvnw_ref[...]).astype(BF16)
        kvv = _dot(ckvn, wkv_ref[...])
        for h in range(n_heads):
            k_out[0, h, rows, :] = (kvv[:, h * LANES:(h + 1) * LANES] + k_rope).astype(BF16)
        vt_out[0, :, rows] = kvv[:, hp:].T.astype(BF16)

    for u, low, rows in zip(us, lows, tiles):
        gate_group(0, u, rows)
        mla_group(low, rows)
    for g in (1, 2, 3):
        for u, rows in zip(us, tiles):
            gate_group(g, u, rows)


def _in_proj(x, cos_t, sin_t, prew, win_ext, qnw, wq_pad, kvnw, wkv_cat, lbl, *, n_heads, hw, tm, n_sub):
    b, s, d = x.shape
    q_rank = qnw.shape[-1]
    kv_rank = kvnw.shape[-1]
    ns = s // tm
    qscale = (MLA_NOPE + MLA_ROPE) ** -0.5 * LOG2E
    kern = functools.partial(_inproj_kernel, qscale=qscale, n_heads=n_heads, q_rank=q_rank,
                             kv_rank=kv_rank, hw=hw, n_sub=n_sub)

    def full(a):
        return pl.BlockSpec(a.shape, lambda i, j: (0,) * a.ndim)

    tok = lambda w: pl.BlockSpec((1, tm, w), lambda i, j: (i, j, 0))
    head = pl.BlockSpec((1, n_heads, tm, LANES), lambda i, j: (i, 0, j, 0))
    head_t = pl.BlockSpec((1, n_heads, LANES, tm), lambda i, j: (i, 0, 0, j))
    tab_t = pl.BlockSpec((cos_t.shape[0], tm), lambda i, j: (0, i * ns + j))
    vw = n_heads * MLA_V
    out_shape = [
        jax.ShapeDtypeStruct((b, n_heads, LANES, s), BF16),
        jax.ShapeDtypeStruct((b, n_heads, s, LANES), BF16),
        jax.ShapeDtypeStruct((b, vw, s), BF16),
        jax.ShapeDtypeStruct((b, s, hw), BF16),
        jax.ShapeDtypeStruct((b, s, hw), F32),
        jax.ShapeDtypeStruct((b, s, hw), BF16),
        jax.ShapeDtypeStruct((b, s, hw), BF16),
        jax.ShapeDtypeStruct((b, s, hw), BF16),
    ]
    return pl.pallas_call(
        kern,
        grid=(b, ns),
        in_specs=[tok(d), tab_t, tab_t, full(prew), full(win_ext), full(qnw), full(wq_pad),
                  full(kvnw), full(wkv_cat), full(lbl)],
        out_specs=[head_t, head, pl.BlockSpec((1, vw, tm), lambda i, j: (i, 0, j)),
                   tok(hw), tok(hw), tok(hw), tok(hw), tok(hw)],
        out_shape=out_shape,
        compiler_params=pltpu.CompilerParams(
            dimension_semantics=("parallel", "parallel"), vmem_limit_bytes=56 * MIB),
        name="in_proj",
    )(x, cos_t, sin_t, prew, win_ext, qnw, wq_pad, kvnw, wkv_cat, lbl)


def _attn_kernel(qt_ref, k_ref, vt_ref, w_ref, o_ref, sa_ref, sb_ref, *, tq, n_q, n_split):
    kc = tq // 2
    qw = tq // n_split
    units = [(h, z) for h in range(2) for z in range(n_split)]
    ones = jnp.ones((ONES_ROWS, kc), BF16)

    def q_tile(qi, _):
        q0 = pl.multiple_of(qi * tq, tq)

        def scores(u, ks, s_ref):
            h, z = units[u]
            s_t = _dot(k_ref[0, h, pl.ds(ks, kc), :], qt_ref[0, h, :, pl.ds(q0 + z * qw, qw)])
            s_ref[u] = s_t
            return jnp.max(s_t, axis=0, keepdims=True)

        def accumulate(u, ks, s_ref, cmax, m, acc, diag_offset=None):
            h, z = units[u]
            s_t = s_ref[u]
            if diag_offset is not None:
                key = lax.broadcasted_iota(jnp.int32, s_t.shape, 0) + diag_offset
                qry = lax.broadcasted_iota(jnp.int32, s_t.shape, 1) + z * qw
                s_t = jnp.where(key <= qry, s_t, -jnp.inf)
                cmax = jnp.max(s_t, axis=0, keepdims=True)
            m_new = jnp.maximum(m, cmax)
            alpha = jnp.exp2(m - m_new)
            p_t = jnp.exp2(s_t - m_new).astype(BF16)
            v_ext = jnp.concatenate(
                [vt_ref[0, h * MLA_V:(h + 1) * MLA_V, pl.ds(ks, kc)], ones], axis=0)
            return m_new, alpha * acc + _dot(v_ext, p_t)

        def pair_body(t, carry):
            k0 = pl.multiple_of(t * tq, tq)
            new = []
            for u in range(len(units)):
                cm_a, m, acc = carry[u]
                cm_b = scores(u, k0 + kc, sb_ref)
                m, acc = accumulate(u, k0, sa_ref, cm_a, m, acc)
                cm_a = scores(u, k0 + tq, sa_ref)
                m, acc = accumulate(u, k0 + kc, sb_ref, cm_b, m, acc)
                new.append((cm_a, m, acc))
            return tuple(new)

        init = tuple((scores(u, 0, sa_ref), jnp.full((1, qw), NEG_BIG, F32),
                      jnp.zeros((MLA_V + ONES_ROWS, qw), F32)) for u in range(len(units)))
        carry = lax.fori_loop(0, qi, pair_body, init)

        outs = []
        for u in range(len(units)):
            cm_a, m, acc = carry[u]
            scores(u, q0 + kc, sb_ref)
            m, acc = accumulate(u, q0, sa_ref, cm_a, m, acc, diag_offset=0)
            m, acc = accumulate(u, q0 + kc, sb_ref, None, m, acc, diag_offset=kc)
            o = acc[:MLA_V] / acc[MLA_V:MLA_V + 1]
            outs.append(o * lax.rsqrt(jnp.mean(o * o, axis=0, keepdims=True) + EPS))
        heads = [jnp.concatenate(outs[h * n_split:(h + 1) * n_split], axis=1) for h in range(2)]
        pair = jnp.concatenate(heads, axis=0).T
        o_ref[0, pl.ds(q0, tq), :] = (pair * w_ref[...]).astype(BF16)
        return 0

    lax.fori_loop(0, n_q, q_tile, 0)


def _mla_attention(qt, k, vt, w_norm, *, tq, n_split):
    b, n_heads, _, s = qt.shape
    kern = functools.partial(_attn_kernel, tq=tq, n_q=s // tq, n_split=n_split)
    s_scratch = pltpu.VMEM((2 * n_split, tq // 2, tq // n_split), F32)
    return pl.pallas_call(
        kern,
        grid=(b, n_heads // 2),
        in_specs=[
            pl.BlockSpec((1, 2, LANES, s), lambda i, j: (i, j, 0, 0)),
            pl.BlockSpec((1, 2, s, LANES), lambda i, j: (i, j, 0, 0)),
            pl.BlockSpec((1, 2 * MLA_V, s), lambda i, j: (i, j, 0)),
            pl.BlockSpec((1, LANES), lambda i, j: (0, j)),
        ],
        out_specs=pl.BlockSpec((1, s, LANES), lambda i, j: (i, 0, j)),
        out_shape=jax.ShapeDtypeStruct((b, s, n_heads * MLA_V), BF16),
        scratch_shapes=[s_scratch, s_scratch],
        compiler_params=pltpu.CompilerParams(
            dimension_semantics=("parallel", "parallel"), vmem_limit_bytes=48 * MIB),
        name="mla_attn",
    )(qt, k, vt, w_norm)


def _hgrn_tables(c, g):
    n_lev = int(np.log2(c))
    t = np.arange(c)[:, None]
    s = np.arange(c)[None, :]
    lev = np.full((c, c), n_lev + 1, np.int32)
    for l in range(n_lev):
        m = c >> l
        half = m // 2
        lev[(t // m == s // m) & (t % m >= half) & (s % m < half)] = l
    lev[t == s] = n_lev
    hc = c // 2
    assert (lev[:hc, :hc] == lev[hc:, hc:]).all() and (lev[hc:, :hc] == 0).all()
    tri = np.tril(np.ones((g, g), np.float32))
    return np.concatenate([tri, tri, tri], axis=1), lev[:hc, :hc], n_lev


def _hgrn_level_exponents(b, lf, c):
    sub = lax.broadcasted_iota(jnp.int32, (F32_SUBLANES, LANES), 0)
    out = []
    m = c
    while m >= 2:
        half = m // 2
        if half >= F32_SUBLANES:
            pieces = []
            for lo in range(0, c, m):
                mid = lo + half
                r = b[mid - 1:mid]
                pieces += [r - b[lo:mid], b[mid:mid + half] - r]
            e = jnp.concatenate(pieces, axis=0)
        elif m == 2:
            odd = lax.broadcasted_iota(jnp.int32, lf.shape, 0) % 2 == 1
            e = jnp.where(odd, lf, 0.0)
        else:
            sign = jnp.where(sub % m >= half, 1.0, -1.0)
            pieces = []
            for v0 in range(0, c, F32_SUBLANES):
                ref = b[v0 + half - 1:v0 + half]
                for j in range(1, F32_SUBLANES // m):
                    ref = jnp.where(sub >= j * m, b[v0 + j * m + half - 1:v0 + j * m + half], ref)
                pieces.append(sign * (b[v0:v0 + F32_SUBLANES] - ref))
            e = jnp.concatenate(pieces, axis=0)
        out.append((m, e))
        m = half
    return out


def _hgrn_kernel(q_ref, k_ref, v_ref, lf_ref, g_ref, w_ref, tri_ref, lev_ref, o_ref,
                 *, c, g, n_blocks, n_hd, n_lev):
    tri = tri_ref[...]
    ng = c // g
    hc = c // 2

    def block(i, states):
        rows = pl.ds(pl.multiple_of(i * c, c), c)
        new_states = []
        for hd in range(n_hd):
            cols = slice(hd * HGRN_HEAD, (hd + 1) * HGRN_HEAD)
            st = states[hd]
            qb = q_ref[0, rows, cols]
            kb = k_ref[0, rows, cols]
            q = qb.astype(F32)
            k = kb.astype(F32)
            v = v_ref[0, rows, cols]
            lf = lf_ref[0, rows, cols]

            lf_w = jnp.concatenate([lf[j * g:(j + 1) * g] for j in range(ng)], axis=1)
            hi = lf_w.astype(BF16)
            r1 = lf_w - hi.astype(F32)
            mid = r1.astype(BF16)
            lo = (r1 - mid.astype(F32)).astype(BF16)
            b_w = _dot(tri, jnp.concatenate([hi, mid, lo], axis=0))
            parts = []
            off = None
            for j in range(ng):
                bj = b_w[:, j * LANES:(j + 1) * LANES]
                if off is not None:
                    bj = bj + off
                parts.append(bj)
                off = bj[g - 1:g]
            b = jnp.concatenate(parts, axis=0)

            lev = lev_ref[...]
            halves = (slice(0, hc), slice(hc, c))
            a_diag = [jnp.zeros((hc, hc), F32), jnp.zeros((hc, hc), F32)]
            a10 = None
            row = lax.broadcasted_iota(jnp.int32, q.shape, 0)
            for l, (m, e) in enumerate(_hgrn_level_exponents(b, lf, c)):
                half = m // 2
                if half >= F32_SUBLANES:
                    mixed = jnp.concatenate(
                        [piece for lo in range(0, c, m) for piece in (k[lo:lo + half], q[lo + half:lo + m])],
                        axis=0)
                else:
                    mixed = jnp.where(row % m >= half, q, k)
                r = (mixed * jnp.exp2(e)).astype(BF16)
                if l == 0:
                    a10 = _dot_nt(r[hc:], r[:hc])
                    continue
                here = lev == l
                for i, rs in enumerate(halves):
                    a_diag[i] = jnp.where(here, _dot_nt(r[rs], r[rs]), a_diag[i])
            here = lev == n_lev
            for i, rs in enumerate(halves):
                a_diag[i] = jnp.where(here, _dot_nt(qb[rs], kb[rs]), a_diag[i])
            a = jnp.concatenate(
                [jnp.concatenate([a_diag[0], jnp.zeros((hc, hc), F32)], axis=1),
                 jnp.concatenate([a10, a_diag[1]], axis=1)], axis=0)

            eb = jnp.exp2(b)
            ek = jnp.exp2(off - b)
            o = _dot_nt((q * eb).astype(BF16), st.astype(BF16)) + _dot(a.astype(BF16), v)
            new_states.append(st * eb[c - 1:c] + _dot_tn(v, (k * ek).astype(BF16)))

            o = _rms(o, w_ref[:, cols]) * g_ref[0, rows, cols].astype(F32)
            o_ref[0, rows, cols] = o.astype(BF16)
        return tuple(new_states)

    init = tuple(jnp.zeros((HGRN_HEAD, HGRN_HEAD), F32) for _ in range(n_hd))
    def trip(t, s):
        for j in range(BLOCKS_PER_TRIP):
            s = block(BLOCKS_PER_TRIP * t + j, s)
        return s

    lax.fori_loop(0, n_blocks // BLOCKS_PER_TRIP, trip, init)


def _hgrn2(hq, kk, hi, lf, g, w_norm, *, heads_per_step):
    b, s, hw = hq.shape
    c = HGRN_BLOCK
    width = heads_per_step * HGRN_HEAD
    tri_np, lev_np, n_lev = _hgrn_tables(c, HGRN_CUMSUM_GROUP)
    tri = jnp.asarray(tri_np, BF16)
    lev = jnp.asarray(lev_np)
    kern = functools.partial(_hgrn_kernel, c=c, g=HGRN_CUMSUM_GROUP, n_blocks=s // c,
                             n_hd=heads_per_step, n_lev=n_lev)
    tok = pl.BlockSpec((1, s, width), lambda i, j: (i, 0, j))
    return pl.pallas_call(
        kern,
        grid=(b, hw // width),
        in_specs=[tok, tok, tok, tok, tok,
                  pl.BlockSpec((1, width), lambda i, j: (0, j)),
                  pl.BlockSpec(tri.shape, lambda i, j: (0, 0)),
                  pl.BlockSpec(lev.shape, lambda i, j: (0, 0))],
        out_specs=tok,
        out_shape=jax.ShapeDtypeStruct((b, s, hw), BF16),
        compiler_params=pltpu.CompilerParams(
            dimension_semantics=("parallel", "parallel"), vmem_limit_bytes=48 * MIB),
        name="hgrn2",
    )(hq, kk, hi, lf, g, w_norm, tri, lev)


def _out_ffn_kernel(om_ref, oh_ref, x_ref, wo_ref, postw_ref, prew_ref, wg_ref, wu_ref, wd_ref,
                    fpostw_ref, out_ref, *, n_sub):
    sub = x_ref.shape[0] // n_sub
    tiles = [slice(i * sub, (i + 1) * sub) for i in range(n_sub)]
    hs = []
    for rows in tiles:
        mix = jnp.concatenate([om_ref[rows, :], oh_ref[rows, :]], axis=-1)
        hs.append(x_ref[rows, :] + _rms(_dot(mix, wo_ref[...]), postw_ref[...]))
    ffs = []
    for h in hs:
        z = _rms(h, prew_ref[...]).astype(BF16)
        gate = _dot(z, wg_ref[...])
        ffs.append((gate * _sigmoid(gate) * _dot(z, wu_ref[...])).astype(BF16))
    for rows, h, ff in zip(tiles, hs, ffs):
        out_ref[rows, :] = h + _rms(_dot(ff, wd_ref[...]), fpostw_ref[...])


def _out_ffn(om, oh, x, wo, postw, prew, wg, wu, wd, fpostw, *, tm, n_sub):
    n_tok, d = x.shape
    kern = functools.partial(_out_ffn_kernel, n_sub=n_sub)

    def full(a):
        return pl.BlockSpec(a.shape, lambda i: (0,) * a.ndim, pipeline_mode=pl.Buffered(1))

    tok = lambda w: pl.BlockSpec((tm, w), lambda i: (i, 0))
    return pl.pallas_call(
        kern,
        grid=(n_tok // tm,),
        in_specs=[tok(om.shape[1]), tok(oh.shape[1]), tok(d), full(wo), full(postw), full(prew),
                  full(wg), full(wu), full(wd), full(fpostw)],
        out_specs=tok(d),
        out_shape=jax.ShapeDtypeStruct((n_tok, d), F32),
        compiler_params=pltpu.CompilerParams(
            dimension_semantics=("parallel",), vmem_limit_bytes=56 * MIB),
        name="out_ffn",
    )(om, oh, x, wo, postw, prew, wg, wu, wd, fpostw)


def _prep_in_proj_weights(w_in, w_uq, w_ukv, *, q_rank, kv_rank):
    d = w_in.shape[0]
    half = MLA_ROPE // 2
    s2 = q_rank + kv_rank
    s3 = s2 + MLA_ROPE
    kr = w_in[:, s2:s3]
    z_lo = jnp.zeros((d, MLA_NOPE), w_in.dtype)
    kr_slab = jnp.concatenate([z_lo, kr, kr[:, half:], kr[:, :half]], axis=1)
    win_ext = jnp.concatenate([w_in[:, :s2], kr_slab, w_in[:, s3:]], axis=1).astype(BF16)

    n_heads = w_uq.shape[1]
    rope = w_uq[:, :, MLA_NOPE:]
    wq_pad = jnp.concatenate([w_uq, rope[:, :, half:], rope[:, :, :half]], axis=2)
    wq_pad = wq_pad.reshape(q_rank, n_heads * LANES).astype(BF16)

    wk_pad = jnp.pad(w_ukv[:, :, :MLA_NOPE], ((0, 0), (0, 0), (0, LANES - MLA_NOPE)))
    wv = w_ukv[:, :, MLA_NOPE:]
    wkv_cat = jnp.concatenate([wk_pad.reshape(kv_rank, n_heads * LANES),
                               wv.reshape(kv_rank, n_heads * MLA_V)], axis=1).astype(BF16)
    return win_ext, wq_pad, wkv_cat


def kernel(x, positions, attn_pre_norm, w_in, mla_q_norm, mla_w_uq, mla_kv_norm, mla_w_ukv, mla_out_norm,
           hgrn_lb_logits, hgrn_out_norm, w_out, attn_post_norm, ffn_pre_norm, w_gate, w_up, w_down,
           ffn_post_norm):
    b, s, d = x.shape
    assert attn_pre_norm.shape[0] == 1, "single-layer block"
    assert mla_w_uq.shape[3] == MLA_NOPE + MLA_ROPE and 2 * MLA_ROPE + MLA_NOPE == LANES
    q_rank = mla_q_norm.shape[-1]
    kv_rank = mla_kv_norm.shape[-1]
    n_heads = mla_w_uq.shape[2]
    hw = hgrn_out_norm.shape[-1]
    row = lambda a: a.reshape(1, -1)

    cos_t, sin_t = _rope_tables(positions)
    win_ext, wq_pad, wkv_cat = _prep_in_proj_weights(w_in[0], mla_w_uq[0], mla_w_ukv[0],
                                                     q_rank=q_rank, kv_rank=kv_rank)
    qt, k, vt, hq, lf, kk, hi, g = _in_proj(
        x, cos_t, sin_t, row(attn_pre_norm[0]), win_ext, row(mla_q_norm[0]), wq_pad,
        row(mla_kv_norm[0]), wkv_cat, hgrn_lb_logits, n_heads=n_heads, hw=hw, tm=512, n_sub=2)
    o_mla = _mla_attention(qt, k, vt, row(mla_out_norm[0]), tq=512, n_split=1)
    o_hgrn = _hgrn2(hq, kk, hi, lf, g, row(hgrn_out_norm[0]), heads_per_step=2)

    out = _out_ffn(o_mla.reshape(b * s, -1), o_hgrn.reshape(b * s, -1), x.reshape(b * s, d),
                   w_out[0].astype(BF16), row(attn_post_norm[0]), row(ffn_pre_norm[0]),
                   w_gate[0].astype(BF16), w_up[0].astype(BF16), w_down[0].astype(BF16),
                   row(ffn_post_norm[0]), tm=512, n_sub=2)
    return out.reshape(b, s, d)
```

```python
import functools
import math

import numpy as np
import jax
import jax.numpy as jnp
from jax import lax
from jax.experimental import pallas as pl
from jax.experimental.pallas import tpu as pltpu

F32 = jnp.float32
BF16 = jnp.bfloat16

EPS = 1e-6
ROPE_THETA = 10000.0
LOG2E = math.log2(math.e)
LANES = 128
BF16_SUBLANES = 16
MLA_NOPE = 64
MLA_ROPE = 32
MLA_V = 64
ONES_ROWS = BF16_SUBLANES
F32_SUBLANES = 8
HGRN_HEAD = 128
HGRN_BLOCK = 256
HGRN_CUMSUM_GROUP = 64
BLOCKS_PER_TRIP = 4
NEG_BIG = -1e30
MIB = 1024 * 1024


def _rms(x, w):
    inv = lax.rsqrt(jnp.mean(x * x, axis=-1, keepdims=True) + EPS)
    return (x * inv) * w


def _sigmoid(x):
    return 1.0 / (1.0 + jnp.exp(-x))


def _dot(a, b):
    return jnp.dot(a, b, preferred_element_type=F32)


def _dot_nt(a, b):
    return lax.dot_general(a, b, (((1,), (1,)), ((), ())), preferred_element_type=F32)


def _dot_tn(a, b):
    return lax.dot_general(a, b, (((0,), (0,)), ((), ())), preferred_element_type=F32)


def _rope_kernel(pos_ref, invf_ref, cos_ref, sin_ref):
    ang = pos_ref[...].astype(F32) * invf_ref[...]
    cos_ref[...] = jnp.cos(ang)
    sin_ref[...] = jnp.sin(ang)


def _rope_tables(positions):
    n_tok = positions.size
    half = MLA_ROPE // 2
    inv_freq = 1.0 / (ROPE_THETA ** (jnp.arange(0, MLA_ROPE, 2, dtype=F32) / MLA_ROPE))
    blk = min(n_tok, 4096)
    spec = pl.BlockSpec((half, blk), lambda i: (0, i))
    shp = jax.ShapeDtypeStruct((half, n_tok), F32)
    return pl.pallas_call(
        _rope_kernel,
        grid=(n_tok // blk,),
        in_specs=[pl.BlockSpec((1, blk), lambda i: (0, i)), pl.BlockSpec((half, 1), lambda i: (0, 0))],
        out_specs=[spec, spec],
        out_shape=[shp, shp],
        name="rope_tables",
    )(positions.reshape(1, n_tok), inv_freq.reshape(half, 1))


def _inproj_kernel(x_ref, ct_ref, st_ref, prew_ref, win_ref, qnw_ref, wq_ref,
                   kvnw_ref, wkv_ref, lbl_ref,
                   qt_out, k_out, vt_out, hq_out, lf_out, kk_out, hi_out, g_out,
                   *, qscale, n_heads, q_rank, kv_rank, hw, n_sub):
    sub = x_ref.shape[1] // n_sub
    tiles = [slice(i * sub, (i + 1) * sub) for i in range(n_sub)]
    r0, r1 = MLA_NOPE, MLA_NOPE + MLA_ROPE
    hp = n_heads * LANES
    c0 = q_rank + kv_rank + LANES

    lbl = lbl_ref[...]
    e = jnp.exp(lbl - jnp.max(lbl, axis=0, keepdims=True))
    lb = e[0:1] / jnp.sum(e, axis=0, keepdims=True)

    us = [_rms(x_ref[0, rows, :], prew_ref[...]).astype(BF16) for rows in tiles]
    lows = [_dot(u, win_ref[:, :c0]) for u in us]

    def gate_group(g, u, rows):
        y = _dot(u, win_ref[:, c0 + g * hw:c0 + (g + 1) * hw])
        if g == 0:
            hq_out[0, rows, :] = (y * _sigmoid(y)).astype(BF16)
        elif g == 1:
            f = lb + (1.0 - lb) * _sigmoid(y)
            lf_out[0, rows, :] = jnp.log(f) * LOG2E
            kk_out[0, rows, :] = (1.0 - f).astype(BF16)
        elif g == 2:
            hi_out[0, rows, :] = y.astype(BF16)
        else:
            g_out[0, rows, :] = (y * _sigmoid(y)).astype(BF16)

    def mla_group(low, rows):
        c_q = low[:, :q_rank]
        c_kv = low[:, q_rank:q_rank + kv_rank]
        kr = low[:, q_rank + kv_rank:]
        ct = ct_ref[:, rows]
        st = st_ref[:, rows]
        cc = jnp.concatenate([ct, ct], axis=0)
        ss = jnp.concatenate([-st, st], axis=0)

        cqn = _rms(c_q, qnw_ref[...]).astype(BF16)
        qq = _dot(cqn, wq_ref[...])
        ccq = cc * qscale
        ssq = ss * qscale
        zpad = jnp.zeros((LANES - r1, sub), F32)
        for h in range(n_heads):
            xt = qq[:, h * LANES:(h + 1) * LANES].T
            rope = xt[r0:r1] * ccq + xt[r1:] * ssq
            qt_out[0, h, :, rows] = jnp.concatenate([xt[:r0] * qscale, rope, zpad], axis=0).astype(BF16)

        z64 = jnp.zeros((r0, sub), F32)
        ck = jnp.concatenate([z64, cc, zpad], axis=0).T
        sk = jnp.concatenate([z64, zpad, ss], axis=0).T
        k_rope = kr * ck + pltpu.roll(kr * sk, LANES - MLA_ROPE, 1)

        ckvn = _rms(c_kv, kvnw_ref[...]).astype(BF16)
        kvv = _dot(ckvn, wkv_ref[...])
        for h in range(n_heads):
            k_out[0, h, rows, :] = (kvv[:, h * LANES:(h + 1) * LANES] + k_rope).astype(BF16)
        vt_out[0, :, rows] = kvv[:, hp:].T.astype(BF16)

    for u, low, rows in zip(us, lows, tiles):
        gate_group(0, u, rows)
        mla_group(low, rows)
    for g in (1, 2, 3):
        for u, rows in zip(us, tiles):
            gate_group(g, u, rows)


def _in_proj(x, cos_t, sin_t, prew, win_ext, qnw, wq_pad, kvnw, wkv_cat, lbl, *, n_heads, hw, tm, n_sub):
    b, s, d = x.shape
    q_rank = qnw.shape[-1]
    kv_rank = kvnw.shape[-1]
    ns = s // tm
    qscale = (MLA_NOPE + MLA_ROPE) ** -0.5 * LOG2E
    kern = functools.partial(_inproj_kernel, qscale=qscale, n_heads=n_heads, q_rank=q_rank,
                             kv_rank=kv_rank, hw=hw, n_sub=n_sub)

    def full(a):
        return pl.BlockSpec(a.shape, lambda i, j: (0,) * a.ndim)

    tok = lambda w: pl.BlockSpec((1, tm, w), lambda i, j: (i, j, 0))
    head = pl.BlockSpec((1, n_heads, tm, LANES), lambda i, j: (i, 0, j, 0))
    head_t = pl.BlockSpec((1, n_heads, LANES, tm), lambda i, j: (i, 0, 0, j))
    tab_t = pl.BlockSpec((cos_t.shape[0], tm), lambda i, j: (0, i * ns + j))
    vw = n_heads * MLA_V
    out_shape = [
        jax.ShapeDtypeStruct((b, n_heads, LANES, s), BF16),
        jax.ShapeDtypeStruct((b, n_heads, s, LANES), BF16),
        jax.ShapeDtypeStruct((b, vw, s), BF16),
        jax.ShapeDtypeStruct((b, s, hw), BF16),
        jax.ShapeDtypeStruct((b, s, hw), F32),
        jax.ShapeDtypeStruct((b, s, hw), BF16),
        jax.ShapeDtypeStruct((b, s, hw), BF16),
        jax.ShapeDtypeStruct((b, s, hw), BF16),
    ]
    return pl.pallas_call(
        kern,
        grid=(b, ns),
        in_specs=[tok(d), tab_t, tab_t, full(prew), full(win_ext), full(qnw), full(wq_pad),
                  full(kvnw), full(wkv_cat), full(lbl)],
        out_specs=[head_t, head, pl.BlockSpec((1, vw, tm), lambda i, j: (i, 0, j)),
                   tok(hw), tok(hw), tok(hw), tok(hw), tok(hw)],
        out_shape=out_shape,
        compiler_params=pltpu.CompilerParams(
            dimension_semantics=("parallel", "parallel"), vmem_limit_bytes=56 * MIB),
        name="in_proj",
    )(x, cos_t, sin_t, prew, win_ext, qnw, wq_pad, kvnw, wkv_cat, lbl)


def _attn_kernel(qt_ref, k_ref, vt_ref, w_ref, o_ref, sa_ref, sb_ref, *, tq, n_q, n_split):
    kc = tq // 2
    qw = tq // n_split
    units = [(h, z) for h in range(2) for z in range(n_split)]
    ones = jnp.ones((ONES_ROWS, kc), BF16)

    nu = range(len(units))

    def scores(u, ks, s_ref, q_start):
        h, z = units[u]
        s_t = _dot(k_ref[0, h, pl.ds(ks, kc), :], qt_ref[0, h, :, pl.ds(q_start + z * qw, qw)])
        s_ref[u] = s_t
        return jnp.max(s_t, axis=0, keepdims=True)

    def q_tile(qi, cm_first):
        q0 = pl.multiple_of(qi * tq, tq)

        def accumulate(u, ks, s_ref, cmax, m, acc, diag_offset=None):
            h, z = units[u]
            s_t = s_ref[u]
            if diag_offset is not None:
                key = lax.broadcasted_iota(jnp.int32, s_t.shape, 0) + diag_offset
                qry = lax.broadcasted_iota(jnp.int32, s_t.shape, 1) + z * qw
                s_t = jnp.where(key <= qry, s_t, -jnp.inf)
                cmax = jnp.max(s_t, axis=0, keepdims=True)
            m_new = jnp.maximum(m, cmax)
            alpha = jnp.exp2(m - m_new)
            p_t = jnp.exp2(s_t - m_new).astype(BF16)
            v_ext = jnp.concatenate(
                [vt_ref[0, h * MLA_V:(h + 1) * MLA_V, pl.ds(ks, kc)], ones], axis=0)
            return m_new, alpha * acc + _dot(v_ext, p_t)

        def pair_body(t, carry):
            k0 = pl.multiple_of(t * tq, tq)
            cm_b = [scores(u, k0 + kc, sb_ref, q0) for u in nu]
            st = [accumulate(u, k0, sa_ref, carry[u][0], carry[u][1], carry[u][2]) for u in nu]
            cm_a = [scores(u, k0 + tq, sa_ref, q0) for u in nu]
            st = [accumulate(u, k0 + kc, sb_ref, cm_b[u], st[u][0], st[u][1]) for u in nu]
            return tuple((cm_a[u], st[u][0], st[u][1]) for u in nu)

        init = tuple((cm_first[u], jnp.full((1, qw), NEG_BIG, F32),
                      jnp.zeros((MLA_V + ONES_ROWS, qw), F32)) for u in nu)
        carry = lax.fori_loop(0, qi // 2, lambda t, c: pair_body(2 * t + 1, pair_body(2 * t, c)), init)
        carry = lax.fori_loop(qi - qi % 2, qi, pair_body, carry)

        for u in nu:
            scores(u, q0 + kc, sb_ref, q0)
        st = [accumulate(u, q0, sa_ref, carry[u][0], carry[u][1], carry[u][2], diag_offset=0) for u in nu]
        q_next = pl.multiple_of(jnp.minimum(qi + 1, n_q - 1) * tq, tq)
        cm_next = [scores(u, 0, sa_ref, q_next) for u in nu[:-1]]
        st = [accumulate(u, q0 + kc, sb_ref, None, st[u][0], st[u][1], diag_offset=kc) for u in nu]
        cm_next = tuple(cm_next + [scores(nu[-1], 0, sa_ref, q_next)])
        outs = []
        for u in nu:
            acc = st[u][1]
            o = acc[:MLA_V] / acc[MLA_V:MLA_V + 1]
            outs.append(o * lax.rsqrt(jnp.mean(o * o, axis=0, keepdims=True) + EPS))
        heads = [jnp.concatenate(outs[h * n_split:(h + 1) * n_split], axis=1) for h in range(2)]
        pair = jnp.concatenate(heads, axis=0).T
        o_ref[0, pl.ds(q0, tq), :] = (pair * w_ref[...]).astype(BF16)
        return cm_next

    lax.fori_loop(0, n_q, q_tile, tuple(scores(u, 0, sa_ref, 0) for u in nu))


def _mla_attention(qt, k, vt, w_norm, *, tq, n_split):
    b, n_heads, _, s = qt.shape
    kern = functools.partial(_attn_kernel, tq=tq, n_q=s // tq, n_split=n_split)
    s_scratch = pltpu.VMEM((2 * n_split, tq // 2, tq // n_split), F32)
    return pl.pallas_call(
        kern,
        grid=(b, n_heads // 2),
        in_specs=[
            pl.BlockSpec((1, 2, LANES, s), lambda i, j: (i, j, 0, 0)),
            pl.BlockSpec((1, 2, s, LANES), lambda i, j: (i, j, 0, 0)),
            pl.BlockSpec((1, 2 * MLA_V, s), lambda i, j: (i, j, 0)),
            pl.BlockSpec((1, LANES), lambda i, j: (0, j)),
        ],
        out_specs=pl.BlockSpec((1, s, LANES), lambda i, j: (i, 0, j)),
        out_shape=jax.ShapeDtypeStruct((b, s, n_heads * MLA_V), BF16),
        scratch_shapes=[s_scratch, s_scratch],
        compiler_params=pltpu.CompilerParams(
            dimension_semantics=("parallel", "parallel"), vmem_limit_bytes=48 * MIB),
        name="mla_attn",
    )(qt, k, vt, w_norm)


def _hgrn_tables(c, g):
    n_lev = int(np.log2(c))
    t = np.arange(c)[:, None]
    s = np.arange(c)[None, :]
    lev = np.full((c, c), n_lev + 1, np.int32)
    for l in range(n_lev):
        m = c >> l
        half = m // 2
        lev[(t // m == s // m) & (t % m >= half) & (s % m < half)] = l
    lev[t == s] = n_lev
    hc = c // 2
    assert (lev[:hc, :hc] == lev[hc:, hc:]).all() and (lev[hc:, :hc] == 0).all()
    tri = np.tril(np.ones((g, g), np.float32))
    return np.concatenate([tri, tri, tri], axis=1), lev[:hc, :hc], n_lev


def _hgrn_level_exponents(b, lf, c):
    sub = lax.broadcasted_iota(jnp.int32, (F32_SUBLANES, LANES), 0)
    out = []
    m = c
    while m >= 2:
        half = m // 2
        if half >= F32_SUBLANES:
            pieces = []
            for lo in range(0, c, m):
                mid = lo + half
                r = b[mid - 1:mid]
                pieces += [r - b[lo:mid], b[mid:mid + half] - r]
            e = jnp.concatenate(pieces, axis=0)
        elif m == 2:
            odd = lax.broadcasted_iota(jnp.int32, lf.shape, 0) % 2 == 1
            e = jnp.where(odd, lf, 0.0)
        else:
            sign = jnp.where(sub % m >= half, 1.0, -1.0)
            pieces = []
            for v0 in range(0, c, F32_SUBLANES):
                ref = b[v0 + half - 1:v0 + half]
                for j in range(1, F32_SUBLANES // m):
                    ref = jnp.where(sub >= j * m, b[v0 + j * m + half - 1:v0 + j * m + half], ref)
                pieces.append(sign * (b[v0:v0 + F32_SUBLANES] - ref))
            e = jnp.concatenate(pieces, axis=0)
        out.append((m, e))
        m = half
    return out


def _hgrn_kernel(q_ref, k_ref, v_ref, lf_ref, g_ref, w_ref, tri_ref, lev_ref, o_ref,
                 *, c, g, n_blocks, n_hd, n_lev):
    tri = tri_ref[...]
    ng = c // g
    hc = c // 2

    def block(i, states):
        rows = pl.ds(pl.multiple_of(i * c, c), c)
        new_states = []
        for hd in range(n_hd):
            cols = slice(hd * HGRN_HEAD, (hd + 1) * HGRN_HEAD)
            st = states[hd]
            qb = q_ref[0, rows, cols]
            kb = k_ref[0, rows, cols]
            q = qb.astype(F32)
            k = kb.astype(F32)
            v = v_ref[0, rows, cols]
            lf = lf_ref[0, rows, cols]

            lf_w = jnp.concatenate([lf[j * g:(j + 1) * g] for j in range(ng)], axis=1)
            hi = lf_w.astype(BF16)
            r1 = lf_w - hi.astype(F32)
            mid = r1.astype(BF16)
            lo = (r1 - mid.astype(F32)).astype(BF16)
            b_w = _dot(tri, jnp.concatenate([hi, mid, lo], axis=0))
            parts = []
            off = None
            for j in range(ng):
                bj = b_w[:, j * LANES:(j + 1) * LANES]
                if off is not None:
                    bj = bj + off
                parts.append(bj)
                off = bj[g - 1:g]
            b = jnp.concatenate(parts, axis=0)

            lev = lev_ref[...]
            halves = (slice(0, hc), slice(hc, c))
            a_diag = [jnp.zeros((hc, hc), F32), jnp.zeros((hc, hc), F32)]
            a10 = None
            row = lax.broadcasted_iota(jnp.int32, q.shape, 0)
            for l, (m, e) in enumerate(_hgrn_level_exponents(b, lf, c)):
                half = m // 2
                if half >= F32_SUBLANES:
                    mixed = jnp.concatenate(
                        [piece for lo in range(0, c, m) for piece in (k[lo:lo + half], q[lo + half:lo + m])],
                        axis=0)
                else:
                    mixed = jnp.where(row % m >= half, q, k)
                r = (mixed * jnp.exp2(e)).astype(BF16)
                if l == 0:
                    a10 = _dot_nt(r[hc:], r[:hc])
                    continue
                here = lev == l
                for i, rs in enumerate(halves):
                    a_diag[i] = jnp.where(here, _dot_nt(r[rs], r[rs]), a_diag[i])
            here = lev == n_lev
            for i, rs in enumerate(halves):
                a_diag[i] = jnp.where(here, _dot_nt(qb[rs], kb[rs]), a_diag[i])
            a = jnp.concatenate(
                [jnp.concatenate([a_diag[0], jnp.zeros((hc, hc), F32)], axis=1),
                 jnp.concatenate([a10, a_diag[1]], axis=1)], axis=0)

            eb = jnp.exp2(b)
            ek = jnp.exp2(off - b)
            o = _dot_nt((q * eb).astype(BF16), st.astype(BF16)) + _dot(a.astype(BF16), v)
            new_states.append(st * eb[c - 1:c] + _dot_tn(v, (k * ek).astype(BF16)))

            o = _rms(o, w_ref[:, cols]) * g_ref[0, rows, cols].astype(F32)
            o_ref[0, rows, cols] = o.astype(BF16)
        return tuple(new_states)

    init = tuple(jnp.zeros((HGRN_HEAD, HGRN_HEAD), F32) for _ in range(n_hd))
    def trip(t, s):
        for j in range(BLOCKS_PER_TRIP):
            s = block(BLOCKS_PER_TRIP * t + j, s)
        return s

    lax.fori_loop(0, n_blocks // BLOCKS_PER_TRIP, trip, init)


def _hgrn2(hq, kk, hi, lf, g, w_norm, *, heads_per_step):
    b, s, hw = hq.shape
    c = HGRN_BLOCK
    width = heads_per_step * HGRN_HEAD
    tri_np, lev_np, n_lev = _hgrn_tables(c, HGRN_CUMSUM_GROUP)
    tri = jnp.asarray(tri_np, BF16)
    lev = jnp.asarray(lev_np)
    kern = functools.partial(_hgrn_kernel, c=c, g=HGRN_CUMSUM_GROUP, n_blocks=s // c,
                             n_hd=heads_per_step, n_lev=n_lev)
    tok = pl.BlockSpec((1, s, width), lambda i, j: (i, 0, j))
    return pl.pallas_call(
        kern,
        grid=(b, hw // width),
        in_specs=[tok, tok, tok, tok, tok,
                  pl.BlockSpec((1, width), lambda i, j: (0, j)),
                  pl.BlockSpec(tri.shape, lambda i, j: (0, 0)),
                  pl.BlockSpec(lev.shape, lambda i, j: (0, 0))],
        out_specs=tok,
        out_shape=jax.ShapeDtypeStruct((b, s, hw), BF16),
        compiler_params=pltpu.CompilerParams(
            dimension_semantics=("parallel", "parallel"), vmem_limit_bytes=48 * MIB),
        name="hgrn2",
    )(hq, kk, hi, lf, g, w_norm, tri, lev)


def _out_ffn_kernel(om_ref, oh_ref, x_ref, wo_ref, postw_ref, prew_ref, wg_ref, wu_ref, wd_ref,
                    fpostw_ref, out_ref, *, n_sub):
    sub = x_ref.shape[0] // n_sub
    tiles = [slice(i * sub, (i + 1) * sub) for i in range(n_sub)]
    hs = []
    for rows in tiles:
        mix = jnp.concatenate([om_ref[rows, :], oh_ref[rows, :]], axis=-1)
        hs.append(x_ref[rows, :] + _rms(_dot(mix, wo_ref[...]), postw_ref[...]))
    ffs = []
    for h in hs:
        z = _rms(h, prew_ref[...]).astype(BF16)
        gate = _dot(z, wg_ref[...])
        ffs.append((gate * _sigmoid(gate) * _dot(z, wu_ref[...])).astype(BF16))
    for rows, h, ff in zip(tiles, hs, ffs):
        out_ref[rows, :] = h + _rms(_dot(ff, wd_ref[...]), fpostw_ref[...])


def _out_ffn(om, oh, x, wo, postw, prew, wg, wu, wd, fpostw, *, tm, n_sub):
    n_tok, d = x.shape
    kern = functools.partial(_out_ffn_kernel, n_sub=n_sub)

    def full(a):
        return pl.BlockSpec(a.shape, lambda i: (0,) * a.ndim, pipeline_mode=pl.Buffered(1))

    tok = lambda w: pl.BlockSpec((tm, w), lambda i: (i, 0))
    return pl.pallas_call(
        kern,
        grid=(n_tok // tm,),
        in_specs=[tok(om.shape[1]), tok(oh.shape[1]), tok(d), full(wo), full(postw), full(prew),
                  full(wg), full(wu), full(wd), full(fpostw)],
        out_specs=tok(d),
        out_shape=jax.ShapeDtypeStruct((n_tok, d), F32),
        compiler_params=pltpu.CompilerParams(
            dimension_semantics=("parallel",), vmem_limit_bytes=56 * MIB),
        name="out_ffn",
    )(om, oh, x, wo, postw, prew, wg, wu, wd, fpostw)


def _prep_in_proj_weights(w_in, w_uq, w_ukv, *, q_rank, kv_rank):
    d = w_in.shape[0]
    half = MLA_ROPE // 2
    s2 = q_rank + kv_rank
    s3 = s2 + MLA_ROPE
    kr = w_in[:, s2:s3]
    z_lo = jnp.zeros((d, MLA_NOPE), w_in.dtype)
    kr_slab = jnp.concatenate([z_lo, kr, kr[:, half:], kr[:, :half]], axis=1)
    win_ext = jnp.concatenate([w_in[:, :s2], kr_slab, w_in[:, s3:]], axis=1).astype(BF16)

    n_heads = w_uq.shape[1]
    rope = w_uq[:, :, MLA_NOPE:]
    wq_pad = jnp.concatenate([w_uq, rope[:, :, half:], rope[:, :, :half]], axis=2)
    wq_pad = wq_pad.reshape(q_rank, n_heads * LANES).astype(BF16)

    wk_pad = jnp.pad(w_ukv[:, :, :MLA_NOPE], ((0, 0), (0, 0), (0, LANES - MLA_NOPE)))
    wv = w_ukv[:, :, MLA_NOPE:]
    wkv_cat = jnp.concatenate([wk_pad.reshape(kv_rank, n_heads * LANES),
                               wv.reshape(kv_rank, n_heads * MLA_V)], axis=1).astype(BF16)
    return win_ext, wq_pad, wkv_cat


def kernel(x, positions, attn_pre_norm, w_in, mla_q_norm, mla_w_uq, mla_kv_norm, mla_w_ukv, mla_out_norm,
           hgrn_lb_logits, hgrn_out_norm, w_out, attn_post_norm, ffn_pre_norm, w_gate, w_up, w_down,
           ffn_post_norm):
    b, s, d = x.shape
    assert attn_pre_norm.shape[0] == 1, "single-layer block"
    assert mla_w_uq.shape[3] == MLA_NOPE + MLA_ROPE and 2 * MLA_ROPE + MLA_NOPE == LANES
    q_rank = mla_q_norm.shape[-1]
    kv_rank = mla_kv_norm.shape[-1]
    n_heads = mla_w_uq.shape[2]
    hw = hgrn_out_norm.shape[-1]
    row = lambda a: a.reshape(1, -1)

    cos_t, sin_t = _rope_tables(positions)
    win_ext, wq_pad, wkv_cat = _prep_in_proj_weights(w_in[0], mla_w_uq[0], mla_w_ukv[0],
                                                     q_rank=q_rank, kv_rank=kv_rank)
    qt, k, vt, hq, lf, kk, hi, g = _in_proj(
        x, cos_t, sin_t, row(attn_pre_norm[0]), win_ext, row(mla_q_norm[0]), wq_pad,
        row(mla_kv_norm[0]), wkv_cat, hgrn_lb_logits, n_heads=n_heads, hw=hw, tm=512, n_sub=2)
    o_mla = _mla_attention(qt, k, vt, row(mla_out_norm[0]), tq=512, n_split=1)
    o_hgrn = _hgrn2(hq, kk, hi, lf, g, row(hgrn_out_norm[0]), heads_per_step=2)

    out = _out_ffn(o_mla.reshape(b * s, -1), o_hgrn.reshape(b * s, -1), x.reshape(b * s, d),
                   w_out[0].astype(BF16), row(attn_post_norm[0]), row(ffn_pre_norm[0]),
                   w_gate[0].astype(BF16), w_up[0].astype(BF16), w_down[0].astype(BF16),
                   row(ffn_post_norm[0]), tm=512, n_sub=2)
    return out.reshape(b, s, d)
```

```python
import functools
import math

import numpy as np
import jax
import jax.numpy as jnp
from jax import lax
from jax.experimental import pallas as pl
from jax.experimental.pallas import tpu as pltpu

F32 = jnp.float32
BF16 = jnp.bfloat16

EPS = 1e-6
ROPE_THETA = 10000.0
LOG2E = math.log2(math.e)
LANES = 128
BF16_SUBLANES = 16
MLA_NOPE = 64
MLA_ROPE = 32
MLA_V = 64
ONES_ROWS = BF16_SUBLANES
F32_SUBLANES = 8
HGRN_HEAD = 128
HGRN_BLOCK = 256
HGRN_CUMSUM_GROUP = 64
BLOCKS_PER_TRIP = 4
NEG_BIG = -1e30
MIB = 1024 * 1024


def _rms(x, w):
    inv = lax.rsqrt(jnp.mean(x * x, axis=-1, keepdims=True) + EPS)
    return (x * inv) * w


def _sigmoid(x):
    return 1.0 / (1.0 + jnp.exp(-x))


def _dot(a, b):
    return jnp.dot(a, b, preferred_element_type=F32)


def _dot_nt(a, b):
    return lax.dot_general(a, b, (((1,), (1,)), ((), ())), preferred_element_type=F32)


def _dot_tn(a, b):
    return lax.dot_general(a, b, (((0,), (0,)), ((), ())), preferred_element_type=F32)


def _rope_kernel(pos_ref, invf_ref, cos_ref, sin_ref):
    ang = pos_ref[...].astype(F32) * invf_ref[...]
    cos_ref[...] = jnp.cos(ang)
    sin_ref[...] = jnp.sin(ang)


def _rope_tables(positions):
    n_tok = positions.size
    half = MLA_ROPE // 2
    inv_freq = 1.0 / (ROPE_THETA ** (jnp.arange(0, MLA_ROPE, 2, dtype=F32) / MLA_ROPE))
    blk = min(n_tok, 4096)
    spec = pl.BlockSpec((half, blk), lambda i: (0, i))
    shp = jax.ShapeDtypeStruct((half, n_tok), F32)
    return pl.pallas_call(
        _rope_kernel,
        grid=(n_tok // blk,),
        in_specs=[pl.BlockSpec((1, blk), lambda i: (0, i)), pl.BlockSpec((half, 1), lambda i: (0, 0))],
        out_specs=[spec, spec],
        out_shape=[shp, shp],
        name="rope_tables",
    )(positions.reshape(1, n_tok), inv_freq.reshape(half, 1))


def _inproj_kernel(x_ref, ct_ref, st_ref, prew_ref, win_ref, qnw_ref, wq_ref,
                   kvnw_ref, wkv_ref, lbl_ref,
                   qt_out, k_out, vt_out, hq_out, lf_out, kk_out, hi_out, g_out,
                   *, qscale, n_heads, q_rank, kv_rank, hw, n_sub):
    sub = x_ref.shape[1] // n_sub
    tiles = [slice(i * sub, (i + 1) * sub) for i in range(n_sub)]
    r0, r1 = MLA_NOPE, MLA_NOPE + MLA_ROPE
    hp = n_heads * LANES
    c0 = q_rank + kv_rank + LANES

    lbl = lbl_ref[...]
    e = jnp.exp(lbl - jnp.max(lbl, axis=0, keepdims=True))
    lb = e[0:1] / jnp.sum(e, axis=0, keepdims=True)

    us = [_rms(x_ref[0, rows, :], prew_ref[...]).astype(BF16) for rows in tiles]
    lows = [_dot(u, win_ref[:, :c0]) for u in us]

    def gate_group(g, u, rows):
        y = _dot(u, win_ref[:, c0 + g * hw:c0 + (g + 1) * hw])
        if g == 0:
            hq_out[0, rows, :] = (y * _sigmoid(y)).astype(BF16)
        elif g == 1:
            f = lb + (1.0 - lb) * _sigmoid(y)
            lf_out[0, rows, :] = jnp.log(f) * LOG2E
            kk_out[0, rows, :] = (1.0 - f).astype(BF16)
        elif g == 2:
            hi_out[0, rows, :] = y.astype(BF16)
        else:
            g_out[0, rows, :] = (y * _sigmoid(y)).astype(BF16)

    def mla_group(low, rows):
        c_q = low[:, :q_rank]
        c_kv = low[:, q_rank:q_rank + kv_rank]
        kr = low[:, q_rank + kv_rank:]
        ct = ct_ref[:, rows]
        st = st_ref[:, rows]
        cc = jnp.concatenate([ct, ct], axis=0)
        ss = jnp.concatenate([-st, st], axis=0)

        cqn = _rms(c_q, qnw_ref[...]).astype(BF16)
        qq = _dot(cqn, wq_ref[...])
        ccq = cc * qscale
        ssq = ss * qscale
        zpad = jnp.zeros((LANES - r1, sub), F32)
        for h in range(n_heads):
            xt = qq[:, h * LANES:(h + 1) * LANES].T
            rope = xt[r0:r1] * ccq + xt[r1:] * ssq
            qt_out[0, h, :, rows] = jnp.concatenate([xt[:r0] * qscale, rope, zpad], axis=0).astype(BF16)

        z64 = jnp.zeros((r0, sub), F32)
        ck = jnp.concatenate([z64, cc, zpad], axis=0).T
        sk = jnp.concatenate([z64, zpad, ss], axis=0).T
        k_rope = kr * ck + pltpu.roll(kr * sk, LANES - MLA_ROPE, 1)

        ckvn = _rms(c_kv, kvnw_ref[...]).astype(BF16)
        kvv = _dot(ckvn, wkv_ref[...])
        for h in range(n_heads):
            k_out[0, h, rows, :] = (kvv[:, h * LANES:(h + 1) * LANES] + k_rope).astype(BF16)
        vt_out[0, :, rows] = kvv[:, hp:].T.astype(BF16)

    for u, low, rows in zip(us, lows, tiles):
        gate_group(0, u, rows)
        mla_group(low, rows)
    for g in (1, 2, 3):
        for u, rows in zip(us, tiles):
            gate_group(g, u, rows)


def _in_proj(x, cos_t, sin_t, prew, win_ext, qnw, wq_pad, kvnw, wkv_cat, lbl, *, n_heads, hw, tm, n_sub):
    b, s, d = x.shape
    q_rank = qnw.shape[-1]
    kv_rank = kvnw.shape[-1]
    ns = s // tm
    qscale = (MLA_NOPE + MLA_ROPE) ** -0.5 * LOG2E
    kern = functools.partial(_inproj_kernel, qscale=qscale, n_heads=n_heads, q_rank=q_rank,
                             kv_rank=kv_rank, hw=hw, n_sub=n_sub)

    def full(a):
        return pl.BlockSpec(a.shape, lambda i, j: (0,) * a.ndim)

    tok = lambda w: pl.BlockSpec((1, tm, w), lambda i, j: (i, j, 0))
    head = pl.BlockSpec((1, n_heads, tm, LANES), lambda i, j: (i, 0, j, 0))
    head_t = pl.BlockSpec((1, n_heads, LANES, tm), lambda i, j: (i, 0, 0, j))
    tab_t = pl.BlockSpec((cos_t.shape[0], tm), lambda i, j: (0, i * ns + j))
    vw = n_heads * MLA_V
    out_shape = [
        jax.ShapeDtypeStruct((b, n_heads, LANES, s), BF16),
        jax.ShapeDtypeStruct((b, n_heads, s, LANES), BF16),
        jax.ShapeDtypeStruct((b, vw, s), BF16),
        jax.ShapeDtypeStruct((b, s, hw), BF16),
        jax.ShapeDtypeStruct((b, s, hw), F32),
        jax.ShapeDtypeStruct((b, s, hw), BF16),
        jax.ShapeDtypeStruct((b, s, hw), BF16),
        jax.ShapeDtypeStruct((b, s, hw), BF16),
    ]
    return pl.pallas_call(
        kern,
        grid=(b, ns),
        in_specs=[tok(d), tab_t, tab_t, full(prew), full(win_ext), full(qnw), full(wq_pad),
                  full(kvnw), full(wkv_cat), full(lbl)],
        out_specs=[head_t, head, pl.BlockSpec((1, vw, tm), lambda i, j: (i, 0, j)),
                   tok(hw), tok(hw), tok(hw), tok(hw), tok(hw)],
        out_shape=out_shape,
        compiler_params=pltpu.CompilerParams(
            dimension_semantics=("parallel", "parallel"), vmem_limit_bytes=56 * MIB),
        name="in_proj",
    )(x, cos_t, sin_t, prew, win_ext, qnw, wq_pad, kvnw, wkv_cat, lbl)


def _attn_kernel(qt_ref, k_ref, vt_ref, w_ref, o_ref, sa_ref, sb_ref, *, tq, n_q, n_split):
    kc = tq // 2
    qw = tq // n_split
    units = [(h, z) for h in range(2) for z in range(n_split)]
    ones = jnp.ones((ONES_ROWS, kc), BF16)

    nu = range(len(units))

    def scores(u, ks, s_ref, q_start):
        h, z = units[u]
        s_t = _dot(k_ref[0, h, pl.ds(ks, kc), :], qt_ref[0, h, :, pl.ds(q_start + z * qw, qw)])
        s_ref[u] = s_t
        return jnp.max(s_t, axis=0, keepdims=True)

    def q_tile(qi, cm_first):
        q0 = pl.multiple_of(qi * tq, tq)

        def accumulate(u, ks, s_ref, cmax, m, acc, diag_offset=None):
            h, z = units[u]
            s_t = s_ref[u]
            if diag_offset is not None:
                key = lax.broadcasted_iota(jnp.int32, s_t.shape, 0) + diag_offset
                qry = lax.broadcasted_iota(jnp.int32, s_t.shape, 1) + z * qw
                s_t = jnp.where(key <= qry, s_t, -jnp.inf)
                cmax = jnp.max(s_t, axis=0, keepdims=True)
            m_new = jnp.maximum(m, cmax)
            alpha = jnp.exp2(m - m_new)
            p_t = jnp.exp2(s_t - m_new).astype(BF16)
            v_ext = jnp.concatenate(
                [vt_ref[0, h * MLA_V:(h + 1) * MLA_V, pl.ds(ks, kc)], ones], axis=0)
            return m_new, alpha * acc + _dot(v_ext, p_t)

        def pair_body(t, carry):
            k0 = pl.multiple_of(t * tq, tq)
            cm_b = [scores(u, k0 + kc, sb_ref, q0) for u in nu]
            st = [accumulate(u, k0, sa_ref, carry[u][0], carry[u][1], carry[u][2]) for u in nu]
            cm_a = [scores(u, k0 + tq, sa_ref, q0) for u in nu]
            st = [accumulate(u, k0 + kc, sb_ref, cm_b[u], st[u][0], st[u][1]) for u in nu]
            return tuple((cm_a[u], st[u][0], st[u][1]) for u in nu)

        init = tuple((cm_first[u], jnp.full((1, qw), NEG_BIG, F32),
                      jnp.zeros((MLA_V + ONES_ROWS, qw), F32)) for u in nu)
        carry = lax.fori_loop(0, qi // 2, lambda t, c: pair_body(2 * t + 1, pair_body(2 * t, c)), init)
        carry = lax.fori_loop(qi - qi % 2, qi, pair_body, carry)

        for u in nu:
            scores(u, q0 + kc, sb_ref, q0)
        st = [accumulate(u, q0, sa_ref, carry[u][0], carry[u][1], carry[u][2], diag_offset=0) for u in nu]
        q_next = pl.multiple_of(jnp.minimum(qi + 1, n_q - 1) * tq, tq)
        cm_next = [scores(u, 0, sa_ref, q_next) for u in nu[:-1]]
        st = [accumulate(u, q0 + kc, sb_ref, None, st[u][0], st[u][1], diag_offset=kc) for u in nu]
        cm_next = tuple(cm_next + [scores(nu[-1], 0, sa_ref, q_next)])
        outs = []
        for u in nu:
            acc = st[u][1]
            o = acc[:MLA_V] / acc[MLA_V:MLA_V + 1]
            outs.append(o * lax.rsqrt(jnp.mean(o * o, axis=0, keepdims=True) + EPS))
        heads = [jnp.concatenate(outs[h * n_split:(h + 1) * n_split], axis=1) for h in range(2)]
        pair = jnp.concatenate(heads, axis=0).T
        o_ref[0, pl.ds(q0, tq), :] = (pair * w_ref[...]).astype(BF16)
        return cm_next

    lax.fori_loop(0, n_q, q_tile, tuple(scores(u, 0, sa_ref, 0) for u in nu))


def _mla_attention(qt, k, vt, w_norm, *, tq, n_split):
    b, n_heads, _, s = qt.shape
    kern = functools.partial(_attn_kernel, tq=tq, n_q=s // tq, n_split=n_split)
    s_scratch = pltpu.VMEM((2 * n_split, tq // 2, tq // n_split), F32)
    return pl.pallas_call(
        kern,
        grid=(b, n_heads // 2),
        in_specs=[
            pl.BlockSpec((1, 2, LANES, s), lambda i, j: (i, j, 0, 0)),
            pl.BlockSpec((1, 2, s, LANES), lambda i, j: (i, j, 0, 0)),
            pl.BlockSpec((1, 2 * MLA_V, s), lambda i, j: (i, j, 0)),
            pl.BlockSpec((1, LANES), lambda i, j: (0, j)),
        ],
        out_specs=pl.BlockSpec((1, s, LANES), lambda i, j: (i, 0, j)),
        out_shape=jax.ShapeDtypeStruct((b, s, n_heads * MLA_V), BF16),
        scratch_shapes=[s_scratch, s_scratch],
        compiler_params=pltpu.CompilerParams(
            dimension_semantics=("parallel", "parallel"), vmem_limit_bytes=48 * MIB),
        name="mla_attn",
    )(qt, k, vt, w_norm)


def _hgrn_tables(c, g):
    n_lev = int(np.log2(c))
    t = np.arange(c)[:, None]
    s = np.arange(c)[None, :]
    lev = np.full((c, c), n_lev + 1, np.int32)
    for l in range(n_lev):
        m = c >> l
        half = m // 2
        lev[(t // m == s // m) & (t % m >= half) & (s % m < half)] = l
    lev[t == s] = n_lev
    hc = c // 2
    assert (lev[:hc, :hc] == lev[hc:, hc:]).all() and (lev[hc:, :hc] == 0).all()
    tri = np.tril(np.ones((g, g), np.float32))
    return np.concatenate([tri, tri, tri], axis=1), lev[:hc, :hc], n_lev


def _hgrn_level_exponents(b, lf, c):
    sub = lax.broadcasted_iota(jnp.int32, (F32_SUBLANES, LANES), 0)
    out = []
    m = c
    while m >= 2:
        half = m // 2
        if half >= F32_SUBLANES:
            pieces = []
            for lo in range(0, c, m):
                mid = lo + half
                r = b[mid - 1:mid]
                pieces += [r - b[lo:mid], b[mid:mid + half] - r]
            e = jnp.concatenate(pieces, axis=0)
        elif m == 2:
            odd = lax.broadcasted_iota(jnp.int32, lf.shape, 0) % 2 == 1
            e = jnp.where(odd, lf, 0.0)
        else:
            sign = jnp.where(sub % m >= half, 1.0, -1.0)
            pieces = []
            for v0 in range(0, c, F32_SUBLANES):
                ref = b[v0 + half - 1:v0 + half]
                for j in range(1, F32_SUBLANES // m):
                    ref = jnp.where(sub >= j * m, b[v0 + j * m + half - 1:v0 + j * m + half], ref)
                pieces.append(sign * (b[v0:v0 + F32_SUBLANES] - ref))
            e = jnp.concatenate(pieces, axis=0)
        out.append((m, e))
        m = half
    return out


def _hgrn_kernel(q_ref, k_ref, v_ref, lf_ref, g_ref, w_ref, tri_ref, lev_ref, o_ref,
                 *, c, g, n_blocks, n_hd, n_lev):
    tri = tri_ref[...]
    ng = c // g
    hc = c // 2

    def where(i, hd):
        return pl.ds(pl.multiple_of(i * c, c), c), slice(hd * HGRN_HEAD, (hd + 1) * HGRN_HEAD)

    def cumulative_gates(i, hd):
        rows, cols = where(i, hd)
        lf = lf_ref[0, rows, cols]
        lf_w = jnp.concatenate([lf[j * g:(j + 1) * g] for j in range(ng)], axis=1)
        hi = lf_w.astype(BF16)
        r1 = lf_w - hi.astype(F32)
        mid = r1.astype(BF16)
        lo = (r1 - mid.astype(F32)).astype(BF16)
        b_w = _dot(tri, jnp.concatenate([hi, mid, lo], axis=0))
        parts = []
        off = None
        for j in range(ng):
            bj = b_w[:, j * LANES:(j + 1) * LANES]
            if off is not None:
                bj = bj + off
            parts.append(bj)
            off = bj[g - 1:g]
        return lf, jnp.concatenate(parts, axis=0), off

    def intra_weights(i, hd, lf, b):
        rows, cols = where(i, hd)
        qb = q_ref[0, rows, cols]
        kb = k_ref[0, rows, cols]
        q = qb.astype(F32)
        k = kb.astype(F32)
        lev = lev_ref[...]
        halves = (slice(0, hc), slice(hc, c))
        a_diag = [jnp.zeros((hc, hc), F32), jnp.zeros((hc, hc), F32)]
        a10 = None
        row = lax.broadcasted_iota(jnp.int32, q.shape, 0)
        for l, (m, e) in enumerate(_hgrn_level_exponents(b, lf, c)):
            half = m // 2
            if half >= F32_SUBLANES:
                mixed = jnp.concatenate(
                    [piece for lo in range(0, c, m) for piece in (k[lo:lo + half], q[lo + half:lo + m])],
                    axis=0)
            else:
                mixed = jnp.where(row % m >= half, q, k)
            r = (mixed * jnp.exp2(e)).astype(BF16)
            if l == 0:
                a10 = _dot_nt(r[hc:], r[:hc])
                continue
            here = lev == l
            for j, rs in enumerate(halves):
                a_diag[j] = jnp.where(here, _dot_nt(r[rs], r[rs]), a_diag[j])
        here = lev == n_lev
        for j, rs in enumerate(halves):
            a_diag[j] = jnp.where(here, _dot_nt(qb[rs], kb[rs]), a_diag[j])
        a = jnp.concatenate(
            [jnp.concatenate([a_diag[0], jnp.zeros((hc, hc), F32)], axis=1),
             jnp.concatenate([a10, a_diag[1]], axis=1)], axis=0)
        return a.astype(BF16)

    def read_out(i, hd, b, b_end, a, st):
        rows, cols = where(i, hd)
        q = q_ref[0, rows, cols].astype(F32)
        k = k_ref[0, rows, cols].astype(F32)
        v = v_ref[0, rows, cols]
        eb = jnp.exp2(b)
        ek = jnp.exp2(b_end - b)
        o = _dot_nt((q * eb).astype(BF16), st.astype(BF16)) + _dot(a, v)
        o = _rms(o, w_ref[:, cols]) * g_ref[0, rows, cols].astype(F32)
        o_ref[0, rows, cols] = o.astype(BF16)
        return st * eb[c - 1:c] + _dot_tn(v, (k * ek).astype(BF16))

    init = tuple(jnp.zeros((HGRN_HEAD, HGRN_HEAD), F32) for _ in range(n_hd))

    def trip(t, states):
        units = [(BLOCKS_PER_TRIP * t + j, hd) for j in range(BLOCKS_PER_TRIP) for hd in range(n_hd)]
        gates = [cumulative_gates(i, hd) for i, hd in units]
        states = list(states)
        pending = None
        for u, (i, hd) in enumerate(units):
            lf, b, b_end = gates[u]
            a = intra_weights(i, hd, lf, b)
            if pending is not None:
                pi, phd, pb, pend, pa = pending
                states[phd] = read_out(pi, phd, pb, pend, pa, states[phd])
            pending = (i, hd, b, b_end, a)
        pi, phd, pb, pend, pa = pending
        states[phd] = read_out(pi, phd, pb, pend, pa, states[phd])
        return tuple(states)

    lax.fori_loop(0, n_blocks // BLOCKS_PER_TRIP, trip, init)


def _hgrn2(hq, kk, hi, lf, g, w_norm, *, heads_per_step):
    b, s, hw = hq.shape
    c = HGRN_BLOCK
    width = heads_per_step * HGRN_HEAD
    tri_np, lev_np, n_lev = _hgrn_tables(c, HGRN_CUMSUM_GROUP)
    tri = jnp.asarray(tri_np, BF16)
    lev = jnp.asarray(lev_np)
    kern = functools.partial(_hgrn_kernel, c=c, g=HGRN_CUMSUM_GROUP, n_blocks=s // c,
                             n_hd=heads_per_step, n_lev=n_lev)
    tok = pl.BlockSpec((1, s, width), lambda i, j: (i, 0, j))
    return pl.pallas_call(
        kern,
        grid=(b, hw // width),
        in_specs=[tok, tok, tok, tok, tok,
                  pl.BlockSpec((1, width), lambda i, j: (0, j)),
                  pl.BlockSpec(tri.shape, lambda i, j: (0, 0)),
                  pl.BlockSpec(lev.shape, lambda i, j: (0, 0))],
        out_specs=tok,
        out_shape=jax.ShapeDtypeStruct((b, s, hw), BF16),
        compiler_params=pltpu.CompilerParams(
            dimension_semantics=("parallel", "parallel"), vmem_limit_bytes=48 * MIB),
        name="hgrn2",
    )(hq, kk, hi, lf, g, w_norm, tri, lev)


def _out_ffn_kernel(om_ref, oh_ref, x_ref, wo_ref, postw_ref, prew_ref, wg_ref, wu_ref, wd_ref,
                    fpostw_ref, out_ref, *, n_sub):
    sub = x_ref.shape[0] // n_sub
    tiles = [slice(i * sub, (i + 1) * sub) for i in range(n_sub)]
    hs = []
    for rows in tiles:
        mix = jnp.concatenate([om_ref[rows, :], oh_ref[rows, :]], axis=-1)
        hs.append(x_ref[rows, :] + _rms(_dot(mix, wo_ref[...]), postw_ref[...]))
    ffs = []
    for h in hs:
        z = _rms(h, prew_ref[...]).astype(BF16)
        gate = _dot(z, wg_ref[...])
        ffs.append((gate * _sigmoid(gate) * _dot(z, wu_ref[...])).astype(BF16))
    for rows, h, ff in zip(tiles, hs, ffs):
        out_ref[rows, :] = h + _rms(_dot(ff, wd_ref[...]), fpostw_ref[...])


def _out_ffn(om, oh, x, wo, postw, prew, wg, wu, wd, fpostw, *, tm, n_sub):
    n_tok, d = x.shape
    kern = functools.partial(_out_ffn_kernel, n_sub=n_sub)

    def full(a):
        return pl.BlockSpec(a.shape, lambda i: (0,) * a.ndim, pipeline_mode=pl.Buffered(1))

    tok = lambda w: pl.BlockSpec((tm, w), lambda i: (i, 0))
    return pl.pallas_call(
        kern,
        grid=(n_tok // tm,),
        in_specs=[tok(om.shape[1]), tok(oh.shape[1]), tok(d), full(wo), full(postw), full(prew),
                  full(wg), full(wu), full(wd), full(fpostw)],
        out_specs=tok(d),
        out_shape=jax.ShapeDtypeStruct((n_tok, d), F32),
        compiler_params=pltpu.CompilerParams(
            dimension_semantics=("parallel",), vmem_limit_bytes=56 * MIB),
        name="out_ffn",
    )(om, oh, x, wo, postw, prew, wg, wu, wd, fpostw)


def _prep_in_proj_weights(w_in, w_uq, w_ukv, *, q_rank, kv_rank):
    d = w_in.shape[0]
    half = MLA_ROPE // 2
    s2 = q_rank + kv_rank
    s3 = s2 + MLA_ROPE
    kr = w_in[:, s2:s3]
    z_lo = jnp.zeros((d, MLA_NOPE), w_in.dtype)
    kr_slab = jnp.concatenate([z_lo, kr, kr[:, half:], kr[:, :half]], axis=1)
    win_ext = jnp.concatenate([w_in[:, :s2], kr_slab, w_in[:, s3:]], axis=1).astype(BF16)

    n_heads = w_uq.shape[1]
    rope = w_uq[:, :, MLA_NOPE:]
    wq_pad = jnp.concatenate([w_uq, rope[:, :, half:], rope[:, :, :half]], axis=2)
    wq_pad = wq_pad.reshape(q_rank, n_heads * LANES).astype(BF16)

    wk_pad = jnp.pad(w_ukv[:, :, :MLA_NOPE], ((0, 0), (0, 0), (0, LANES - MLA_NOPE)))
    wv = w_ukv[:, :, MLA_NOPE:]
    wkv_cat = jnp.concatenate([wk_pad.reshape(kv_rank, n_heads * LANES),
                               wv.reshape(kv_rank, n_heads * MLA_V)], axis=1).astype(BF16)
    return win_ext, wq_pad, wkv_cat


def kernel(x, positions, attn_pre_norm, w_in, mla_q_norm, mla_w_uq, mla_kv_norm, mla_w_ukv, mla_out_norm,
           hgrn_lb_logits, hgrn_out_norm, w_out, attn_post_norm, ffn_pre_norm, w_gate, w_up, w_down,
           ffn_post_norm):
    b, s, d = x.shape
    assert attn_pre_norm.shape[0] == 1, "single-layer block"
    assert mla_w_uq.shape[3] == MLA_NOPE + MLA_ROPE and 2 * MLA_ROPE + MLA_NOPE == LANES
    q_rank = mla_q_norm.shape[-1]
    kv_rank = mla_kv_norm.shape[-1]
    n_heads = mla_w_uq.shape[2]
    hw = hgrn_out_norm.shape[-1]
    row = lambda a: a.reshape(1, -1)

    cos_t, sin_t = _rope_tables(positions)
    win_ext, wq_pad, wkv_cat = _prep_in_proj_weights(w_in[0], mla_w_uq[0], mla_w_ukv[0],
                                                     q_rank=q_rank, kv_rank=kv_rank)
    qt, k, vt, hq, lf, kk, hi, g = _in_proj(
        x, cos_t, sin_t, row(attn_pre_norm[0]), win_ext, row(mla_q_norm[0]), wq_pad,
        row(mla_kv_norm[0]), wkv_cat, hgrn_lb_logits, n_heads=n_heads, hw=hw, tm=512, n_sub=2)
    o_mla = _mla_attention(qt, k, vt, row(mla_out_norm[0]), tq=512, n_split=1)
    o_hgrn = _hgrn2(hq, kk, hi, lf, g, row(hgrn_out_norm[0]), heads_per_step=2)

    out = _out_ffn(o_mla.reshape(b * s, -1), o_hgrn.reshape(b * s, -1), x.reshape(b * s, d),
                   w_out[0].astype(BF16), row(attn_post_norm[0]), row(ffn_pre_norm[0]),
                   w_gate[0].astype(BF16), w_up[0].astype(BF16), w_down[0].astype(BF16),
                   row(ffn_post_norm[0]), tm=512, n_sub=2)
    return out.reshape(b, s, d)
```

```python
import functools
import math

import numpy as np
import jax
import jax.numpy as jnp
from jax import lax
from jax.experimental import pallas as pl
from jax.experimental.pallas import tpu as pltpu

F32 = jnp.float32
BF16 = jnp.bfloat16

EPS = 1e-6
ROPE_THETA = 10000.0
LOG2E = math.log2(math.e)
LANES = 128
BF16_SUBLANES = 16
MLA_NOPE = 64
MLA_ROPE = 32
MLA_V = 64
ONES_ROWS = BF16_SUBLANES
F32_SUBLANES = 8
HGRN_HEAD = 128
HGRN_BLOCK = 256
HGRN_CUMSUM_GROUP = 64
BLOCKS_PER_TRIP = 4
NEG_BIG = -1e30
MIB = 1024 * 1024


def _rms(x, w):
    inv = lax.rsqrt(jnp.mean(x * x, axis=-1, keepdims=True) + EPS)
    return (x * inv) * w


def _sigmoid(x):
    return 1.0 / (1.0 + jnp.exp(-x))


def _dot(a, b):
    return jnp.dot(a, b, preferred_element_type=F32)


def _dot_nt(a, b):
    return lax.dot_general(a, b, (((1,), (1,)), ((), ())), preferred_element_type=F32)


def _dot_tn(a, b):
    return lax.dot_general(a, b, (((0,), (0,)), ((), ())), preferred_element_type=F32)


def _rope_kernel(pos_ref, invf_ref, cos_ref, sin_ref):
    ang = pos_ref[...].astype(F32) * invf_ref[...]
    cos_ref[...] = jnp.cos(ang)
    sin_ref[...] = jnp.sin(ang)


def _rope_tables(positions):
    n_tok = positions.size
    half = MLA_ROPE // 2
    inv_freq = 1.0 / (ROPE_THETA ** (jnp.arange(0, MLA_ROPE, 2, dtype=F32) / MLA_ROPE))
    blk = min(n_tok, 4096)
    spec = pl.BlockSpec((half, blk), lambda i: (0, i))
    shp = jax.ShapeDtypeStruct((half, n_tok), F32)
    return pl.pallas_call(
        _rope_kernel,
        grid=(n_tok // blk,),
        in_specs=[pl.BlockSpec((1, blk), lambda i: (0, i)), pl.BlockSpec((half, 1), lambda i: (0, 0))],
        out_specs=[spec, spec],
        out_shape=[shp, shp],
        name="rope_tables",
    )(positions.reshape(1, n_tok), inv_freq.reshape(half, 1))


def _inproj_kernel(x_ref, ct_ref, st_ref, prew_ref, win_ref, qnw_ref, wq_ref,
                   kvnw_ref, wkv_ref, lbl_ref,
                   qt_out, k_out, vt_out, hq_out, lf_out, kk_out, hi_out, g_out,
                   *, qscale, n_heads, q_rank, kv_rank, hw, n_sub):
    sub = x_ref.shape[1] // n_sub
    tiles = [slice(i * sub, (i + 1) * sub) for i in range(n_sub)]
    r0, r1 = MLA_NOPE, MLA_NOPE + MLA_ROPE
    hp = n_heads * LANES
    c0 = q_rank + kv_rank + LANES

    lbl = lbl_ref[...]
    e = jnp.exp(lbl - jnp.max(lbl, axis=0, keepdims=True))
    lb = e[0:1] / jnp.sum(e, axis=0, keepdims=True)

    us = [_rms(x_ref[0, rows, :], prew_ref[...]).astype(BF16) for rows in tiles]
    lows = [_dot(u, win_ref[:, :c0]) for u in us]

    def gate_group(g, u, rows):
        y = _dot(u, win_ref[:, c0 + g * hw:c0 + (g + 1) * hw])
        if g == 0:
            hq_out[0, rows, :] = (y * _sigmoid(y)).astype(BF16)
        elif g == 1:
            f = lb + (1.0 - lb) * _sigmoid(y)
            lf_out[0, rows, :] = jnp.log(f) * LOG2E
            kk_out[0, rows, :] = (1.0 - f).astype(BF16)
        elif g == 2:
            hi_out[0, rows, :] = y.astype(BF16)
        else:
            g_out[0, rows, :] = (y * _sigmoid(y)).astype(BF16)

    def mla_group(low, rows):
        c_q = low[:, :q_rank]
        c_kv = low[:, q_rank:q_rank + kv_rank]
        kr = low[:, q_rank + kv_rank:]
        ct = ct_ref[:, rows]
        st = st_ref[:, rows]
        cc = jnp.concatenate([ct, ct], axis=0)
        ss = jnp.concatenate([-st, st], axis=0)

        cqn = _rms(c_q, qnw_ref[...]).astype(BF16)
        qq = _dot(cqn, wq_ref[...])
        ccq = cc * qscale
        ssq = ss * qscale
        zpad = jnp.zeros((LANES - r1, sub), F32)
        for h in range(n_heads):
            xt = qq[:, h * LANES:(h + 1) * LANES].T
            rope = xt[r0:r1] * ccq + xt[r1:] * ssq
            qt_out[0, h, :, rows] = jnp.concatenate([xt[:r0] * qscale, rope, zpad], axis=0).astype(BF16)

        z64 = jnp.zeros((r0, sub), F32)
        ck = jnp.concatenate([z64, cc, zpad], axis=0).T
        sk = jnp.concatenate([z64, zpad, ss], axis=0).T
        k_rope = kr * ck + pltpu.roll(kr * sk, LANES - MLA_ROPE, 1)

        ckvn = _rms(c_kv, kvnw_ref[...]).astype(BF16)
        kvv = _dot(ckvn, wkv_ref[...])
        for h in range(n_heads):
            k_out[0, h, rows, :] = (kvv[:, h * LANES:(h + 1) * LANES] + k_rope).astype(BF16)
        vt_out[0, :, rows] = kvv[:, hp:].T.astype(BF16)

    for u, low, rows in zip(us, lows, tiles):
        gate_group(0, u, rows)
        mla_group(low, rows)
    for g in (1, 2, 3):
        for u, rows in zip(us, tiles):
            gate_group(g, u, rows)


def _in_proj(x, cos_t, sin_t, prew, win_ext, qnw, wq_pad, kvnw, wkv_cat, lbl, *, n_heads, hw, tm, n_sub):
    b, s, d = x.shape
    q_rank = qnw.shape[-1]
    kv_rank = kvnw.shape[-1]
    ns = s // tm
    qscale = (MLA_NOPE + MLA_ROPE) ** -0.5 * LOG2E
    kern = functools.partial(_inproj_kernel, qscale=qscale, n_heads=n_heads, q_rank=q_rank,
                             kv_rank=kv_rank, hw=hw, n_sub=n_sub)

    def full(a):
        return pl.BlockSpec(a.shape, lambda i, j: (0,) * a.ndim)

    tok = lambda w: pl.BlockSpec((1, tm, w), lambda i, j: (i, j, 0))
    head = pl.BlockSpec((1, n_heads, tm, LANES), lambda i, j: (i, 0, j, 0))
    head_t = pl.BlockSpec((1, n_heads, LANES, tm), lambda i, j: (i, 0, 0, j))
    tab_t = pl.BlockSpec((cos_t.shape[0], tm), lambda i, j: (0, i * ns + j))
    vw = n_heads * MLA_V
    out_shape = [
        jax.ShapeDtypeStruct((b, n_heads, LANES, s), BF16),
        jax.ShapeDtypeStruct((b, n_heads, s, LANES), BF16),
        jax.ShapeDtypeStruct((b, vw, s), BF16),
        jax.ShapeDtypeStruct((b, s, hw), BF16),
        jax.ShapeDtypeStruct((b, s, hw), F32),
        jax.ShapeDtypeStruct((b, s, hw), BF16),
        jax.ShapeDtypeStruct((b, s, hw), BF16),
        jax.ShapeDtypeStruct((b, s, hw), BF16),
    ]
    return pl.pallas_call(
        kern,
        grid=(b, ns),
        in_specs=[tok(d), tab_t, tab_t, full(prew), full(win_ext), full(qnw), full(wq_pad),
                  full(kvnw), full(wkv_cat), full(lbl)],
        out_specs=[head_t, head, pl.BlockSpec((1, vw, tm), lambda i, j: (i, 0, j)),
                   tok(hw), tok(hw), tok(hw), tok(hw), tok(hw)],
        out_shape=out_shape,
        compiler_params=pltpu.CompilerParams(
            dimension_semantics=("parallel", "parallel"), vmem_limit_bytes=56 * MIB),
        name="in_proj",
    )(x, cos_t, sin_t, prew, win_ext, qnw, wq_pad, kvnw, wkv_cat, lbl)


def _attn_kernel(qt_ref, k_ref, vt_ref, w_ref, o_ref, sa_ref, sb_ref, *, tq, n_q):
    kc = tq // 2
    ones = jnp.ones((ONES_ROWS, kc), BF16)
    nu = range(2)

    def scores(u, ks, s_ref, q_start):
        s_t = _dot(k_ref[0, u, pl.ds(ks, kc), :], qt_ref[0, u, :, pl.ds(q_start, tq)])
        s_ref[u] = s_t
        return jnp.max(s_t, axis=0, keepdims=True)

    def scores_right_half(u, ks, s_ref, q_start):
        s_ref[u, :, kc:] = _dot(k_ref[0, u, pl.ds(ks, kc), :], qt_ref[0, u, :, pl.ds(q_start + kc, kc)])

    def q_tile(qi, cm_first):
        q0 = pl.multiple_of(qi * tq, tq)

        def values(u, ks):
            return jnp.concatenate([vt_ref[0, u * MLA_V:(u + 1) * MLA_V, pl.ds(ks, kc)], ones], axis=0)

        def causal(s_sq):
            key = lax.broadcasted_iota(jnp.int32, s_sq.shape, 0)
            qry = lax.broadcasted_iota(jnp.int32, s_sq.shape, 1)
            return jnp.where(key <= qry, s_sq, -jnp.inf)

        def update(s_t, cmax, m, acc, v_ext):
            m_new = jnp.maximum(m, cmax)
            p_t = jnp.exp2(s_t - m_new).astype(BF16)
            return m_new, jnp.exp2(m - m_new) * acc + _dot(v_ext, p_t)

        def accumulate(u, ks, s_ref, cmax, m, acc):
            return update(s_ref[u], cmax, m, acc, values(u, ks))

        def accumulate_first_diagonal(u, s_ref, cmax, m, acc):
            s_t = s_ref[u]
            left = causal(s_t[:, :kc])
            s_t = jnp.concatenate([left, s_t[:, kc:]], axis=1)
            cmax = jnp.concatenate([jnp.max(left, axis=0, keepdims=True), cmax[:, kc:]], axis=1)
            return update(s_t, cmax, m, acc, values(u, q0))

        def accumulate_second_diagonal(u, s_ref, m, acc):
            s_t = causal(s_ref[u, :, kc:])
            m_r, acc_r = update(s_t, jnp.max(s_t, axis=0, keepdims=True), m[:, kc:], acc[:, kc:],
                                values(u, q0 + kc))
            return (jnp.concatenate([m[:, :kc], m_r], axis=1),
                    jnp.concatenate([acc[:, :kc], acc_r], axis=1))

        def pair_body(t, carry):
            k0 = pl.multiple_of(t * tq, tq)
            cm_b = [scores(u, k0 + kc, sb_ref, q0) for u in nu]
            st = [accumulate(u, k0, sa_ref, carry[u][0], carry[u][1], carry[u][2]) for u in nu]
            cm_a = [scores(u, k0 + tq, sa_ref, q0) for u in nu]
            st = [accumulate(u, k0 + kc, sb_ref, cm_b[u], st[u][0], st[u][1]) for u in nu]
            return tuple((cm_a[u], st[u][0], st[u][1]) for u in nu)

        init = tuple((cm_first[u], jnp.full((1, tq), NEG_BIG, F32),
                      jnp.zeros((MLA_V + ONES_ROWS, tq), F32)) for u in nu)
        carry = lax.fori_loop(0, qi // 2, lambda t, c: pair_body(2 * t + 1, pair_body(2 * t, c)), init)
        carry = lax.fori_loop(qi - qi % 2, qi, pair_body, carry)

        for u in nu:
            scores_right_half(u, q0 + kc, sb_ref, q0)
        st = [accumulate_first_diagonal(u, sa_ref, carry[u][0], carry[u][1], carry[u][2]) for u in nu]
        q_next = pl.multiple_of(jnp.minimum(qi + 1, n_q - 1) * tq, tq)
        cm_next = [scores(u, 0, sa_ref, q_next) for u in nu[:-1]]
        st = [accumulate_second_diagonal(u, sb_ref, st[u][0], st[u][1]) for u in nu]
        cm_next = tuple(cm_next + [scores(nu[-1], 0, sa_ref, q_next)])
        outs = []
        for u in nu:
            acc = st[u][1]
            o = acc[:MLA_V] / acc[MLA_V:MLA_V + 1]
            outs.append(o * lax.rsqrt(jnp.mean(o * o, axis=0, keepdims=True) + EPS))
        pair = jnp.concatenate(outs, axis=0).T
        o_ref[0, pl.ds(q0, tq), :] = (pair * w_ref[...]).astype(BF16)
        return cm_next

    lax.fori_loop(0, n_q, q_tile, tuple(scores(u, 0, sa_ref, 0) for u in nu))


def _mla_attention(qt, k, vt, w_norm, *, tq):
    b, n_heads, _, s = qt.shape
    kern = functools.partial(_attn_kernel, tq=tq, n_q=s // tq)
    s_scratch = pltpu.VMEM((2, tq // 2, tq), F32)
    return pl.pallas_call(
        kern,
        grid=(b, n_heads // 2),
        in_specs=[
            pl.BlockSpec((1, 2, LANES, s), lambda i, j: (i, j, 0, 0)),
            pl.BlockSpec((1, 2, s, LANES), lambda i, j: (i, j, 0, 0)),
            pl.BlockSpec((1, 2 * MLA_V, s), lambda i, j: (i, j, 0)),
            pl.BlockSpec((1, LANES), lambda i, j: (0, j)),
        ],
        out_specs=pl.BlockSpec((1, s, LANES), lambda i, j: (i, 0, j)),
        out_shape=jax.ShapeDtypeStruct((b, s, n_heads * MLA_V), BF16),
        scratch_shapes=[s_scratch, s_scratch],
        compiler_params=pltpu.CompilerParams(
            dimension_semantics=("parallel", "parallel"), vmem_limit_bytes=48 * MIB),
        name="mla_attn",
    )(qt, k, vt, w_norm)


def _hgrn_tables(c, g):
    n_lev = int(np.log2(c))
    t = np.arange(c)[:, None]
    s = np.arange(c)[None, :]
    lev = np.full((c, c), n_lev + 1, np.int32)
    for l in range(n_lev):
        m = c >> l
        half = m // 2
        lev[(t // m == s // m) & (t % m >= half) & (s % m < half)] = l
    lev[t == s] = n_lev
    hc = c // 2
    assert (lev[:hc, :hc] == lev[hc:, hc:]).all() and (lev[hc:, :hc] == 0).all()
    tri = np.tril(np.ones((g, g), np.float32))
    return np.concatenate([tri, tri, tri], axis=1), lev[:hc, :hc], n_lev


def _hgrn_level_exponents(b, lf, c):
    sub = lax.broadcasted_iota(jnp.int32, (F32_SUBLANES, LANES), 0)
    out = []
    m = c
    while m >= 2:
        half = m // 2
        if half >= F32_SUBLANES:
            pieces = []
            for lo in range(0, c, m):
                mid = lo + half
                r = b[mid - 1:mid]
                pieces += [r - b[lo:mid], b[mid:mid + half] - r]
            e = jnp.concatenate(pieces, axis=0)
        elif m == 2:
            odd = lax.broadcasted_iota(jnp.int32, lf.shape, 0) % 2 == 1
            e = jnp.where(odd, lf, 0.0)
        else:
            sign = jnp.where(sub % m >= half, 1.0, -1.0)
            pieces = []
            for v0 in range(0, c, F32_SUBLANES):
                ref = b[v0 + half - 1:v0 + half]
                for j in range(1, F32_SUBLANES // m):
                    ref = jnp.where(sub >= j * m, b[v0 + j * m + half - 1:v0 + j * m + half], ref)
                pieces.append(sign * (b[v0:v0 + F32_SUBLANES] - ref))
            e = jnp.concatenate(pieces, axis=0)
        out.append((m, e))
        m = half
    return out


def _hgrn_kernel(q_ref, k_ref, v_ref, lf_ref, g_ref, w_ref, tri_ref, lev_ref, o_ref,
                 *, c, g, n_blocks, n_hd, n_lev):
    tri = tri_ref[...]
    ng = c // g
    hc = c // 2

    def where(i, hd):
        return pl.ds(pl.multiple_of(i * c, c), c), slice(hd * HGRN_HEAD, (hd + 1) * HGRN_HEAD)

    def cumulative_gates(i, hd):
        rows, cols = where(i, hd)
        lf = lf_ref[0, rows, cols]
        lf_w = jnp.concatenate([lf[j * g:(j + 1) * g] for j in range(ng)], axis=1)
        hi = lf_w.astype(BF16)
        r1 = lf_w - hi.astype(F32)
        mid = r1.astype(BF16)
        lo = (r1 - mid.astype(F32)).astype(BF16)
        b_w = _dot(tri, jnp.concatenate([hi, mid, lo], axis=0))
        parts = []
        off = None
        for j in range(ng):
            bj = b_w[:, j * LANES:(j + 1) * LANES]
            if off is not None:
                bj = bj + off
            parts.append(bj)
            off = bj[g - 1:g]
        return lf, jnp.concatenate(parts, axis=0), off

    def intra_weights(i, hd, lf, b):
        rows, cols = where(i, hd)
        qb = q_ref[0, rows, cols]
        kb = k_ref[0, rows, cols]
        q = qb.astype(F32)
        k = kb.astype(F32)
        lev = lev_ref[...]
        halves = (slice(0, hc), slice(hc, c))
        a_diag = [jnp.zeros((hc, hc), F32), jnp.zeros((hc, hc), F32)]
        a10 = None
        row = lax.broadcasted_iota(jnp.int32, q.shape, 0)
        for l, (m, e) in enumerate(_hgrn_level_exponents(b, lf, c)):
            half = m // 2
            if half >= F32_SUBLANES:
                mixed = jnp.concatenate(
                    [piece for lo in range(0, c, m) for piece in (k[lo:lo + half], q[lo + half:lo + m])],
                    axis=0)
            else:
                mixed = jnp.where(row % m >= half, q, k)
            r = (mixed * jnp.exp2(e)).astype(BF16)
            if l == 0:
                a10 = _dot_nt(r[hc:], r[:hc])
                continue
            here = lev == l
            for j, rs in enumerate(halves):
                a_diag[j] = jnp.where(here, _dot_nt(r[rs], r[rs]), a_diag[j])
        here = lev == n_lev
        for j, rs in enumerate(halves):
            a_diag[j] = jnp.where(here, _dot_nt(qb[rs], kb[rs]), a_diag[j])
        a = jnp.concatenate(
            [jnp.concatenate([a_diag[0], jnp.zeros((hc, hc), F32)], axis=1),
             jnp.concatenate([a10, a_diag[1]], axis=1)], axis=0)
        return a.astype(BF16)

    def read_out(i, hd, b, b_end, a, st):
        rows, cols = where(i, hd)
        q = q_ref[0, rows, cols].astype(F32)
        k = k_ref[0, rows, cols].astype(F32)
        v = v_ref[0, rows, cols]
        eb = jnp.exp2(b)
        ek = jnp.exp2(b_end - b)
        o = _dot_nt((q * eb).astype(BF16), st.astype(BF16)) + _dot(a, v)
        o = _rms(o, w_ref[:, cols]) * g_ref[0, rows, cols].astype(F32)
        o_ref[0, rows, cols] = o.astype(BF16)
        return st * eb[c - 1:c] + _dot_tn(v, (k * ek).astype(BF16))

    init = tuple(jnp.zeros((HGRN_HEAD, HGRN_HEAD), F32) for _ in range(n_hd))

    def trip(t, states):
        units = [(BLOCKS_PER_TRIP * t + j, hd) for j in range(BLOCKS_PER_TRIP) for hd in range(n_hd)]
        gates = [cumulative_gates(i, hd) for i, hd in units]
        states = list(states)
        pending = None
        for u, (i, hd) in enumerate(units):
            lf, b, b_end = gates[u]
            a = intra_weights(i, hd, lf, b)
            if pending is not None:
                pi, phd, pb, pend, pa = pending
                states[phd] = read_out(pi, phd, pb, pend, pa, states[phd])
            pending = (i, hd, b, b_end, a)
        pi, phd, pb, pend, pa = pending
        states[phd] = read_out(pi, phd, pb, pend, pa, states[phd])
        return tuple(states)

    lax.fori_loop(0, n_blocks // BLOCKS_PER_TRIP, trip, init)


def _hgrn2(hq, kk, hi, lf, g, w_norm, *, heads_per_step):
    b, s, hw = hq.shape
    c = HGRN_BLOCK
    width = heads_per_step * HGRN_HEAD
    tri_np, lev_np, n_lev = _hgrn_tables(c, HGRN_CUMSUM_GROUP)
    tri = jnp.asarray(tri_np, BF16)
    lev = jnp.asarray(lev_np)
    kern = functools.partial(_hgrn_kernel, c=c, g=HGRN_CUMSUM_GROUP, n_blocks=s // c,
                             n_hd=heads_per_step, n_lev=n_lev)
    tok = pl.BlockSpec((1, s, width), lambda i, j: (i, 0, j))
    return pl.pallas_call(
        kern,
        grid=(b, hw // width),
        in_specs=[tok, tok, tok, tok, tok,
                  pl.BlockSpec((1, width), lambda i, j: (0, j)),
                  pl.BlockSpec(tri.shape, lambda i, j: (0, 0)),
                  pl.BlockSpec(lev.shape, lambda i, j: (0, 0))],
        out_specs=tok,
        out_shape=jax.ShapeDtypeStruct((b, s, hw), BF16),
        compiler_params=pltpu.CompilerParams(
            dimension_semantics=("parallel", "parallel"), vmem_limit_bytes=48 * MIB),
        name="hgrn2",
    )(hq, kk, hi, lf, g, w_norm, tri, lev)


def _out_ffn_kernel(om_ref, oh_ref, x_ref, wo_ref, postw_ref, prew_ref, wg_ref, wu_ref, wd_ref,
                    fpostw_ref, out_ref, *, n_sub):
    sub = x_ref.shape[0] // n_sub
    tiles = [slice(i * sub, (i + 1) * sub) for i in range(n_sub)]
    hs = []
    for rows in tiles:
        mix = jnp.concatenate([om_ref[rows, :], oh_ref[rows, :]], axis=-1)
        hs.append(x_ref[rows, :] + _rms(_dot(mix, wo_ref[...]), postw_ref[...]))
    ffs = []
    for h in hs:
        z = _rms(h, prew_ref[...]).astype(BF16)
        gate = _dot(z, wg_ref[...])
        ffs.append((gate * _sigmoid(gate) * _dot(z, wu_ref[...])).astype(BF16))
    for rows, h, ff in zip(tiles, hs, ffs):
        out_ref[rows, :] = h + _rms(_dot(ff, wd_ref[...]), fpostw_ref[...])


def _out_ffn(om, oh, x, wo, postw, prew, wg, wu, wd, fpostw, *, tm, n_sub):
    n_tok, d = x.shape
    kern = functools.partial(_out_ffn_kernel, n_sub=n_sub)

    def full(a):
        return pl.BlockSpec(a.shape, lambda i: (0,) * a.ndim, pipeline_mode=pl.Buffered(1))

    tok = lambda w: pl.BlockSpec((tm, w), lambda i: (i, 0))
    return pl.pallas_call(
        kern,
        grid=(n_tok // tm,),
        in_specs=[tok(om.shape[1]), tok(oh.shape[1]), tok(d), full(wo), full(postw), full(prew),
                  full(wg), full(wu), full(wd), full(fpostw)],
        out_specs=tok(d),
        out_shape=jax.ShapeDtypeStruct((n_tok, d), F32),
        compiler_params=pltpu.CompilerParams(
            dimension_semantics=("parallel",), vmem_limit_bytes=56 * MIB),
        name="out_ffn",
    )(om, oh, x, wo, postw, prew, wg, wu, wd, fpostw)


def _prep_in_proj_weights(w_in, w_uq, w_ukv, *, q_rank, kv_rank):
    d = w_in.shape[0]
    half = MLA_ROPE // 2
    s2 = q_rank + kv_rank
    s3 = s2 + MLA_ROPE
    kr = w_in[:, s2:s3]
    z_lo = jnp.zeros((d, MLA_NOPE), w_in.dtype)
    kr_slab = jnp.concatenate([z_lo, kr, kr[:, half:], kr[:, :half]], axis=1)
    win_ext = jnp.concatenate([w_in[:, :s2], kr_slab, w_in[:, s3:]], axis=1).astype(BF16)

    n_heads = w_uq.shape[1]
    rope = w_uq[:, :, MLA_NOPE:]
    wq_pad = jnp.concatenate([w_uq, rope[:, :, half:], rope[:, :, :half]], axis=2)
    wq_pad = wq_pad.reshape(q_rank, n_heads * LANES).astype(BF16)

    wk_pad = jnp.pad(w_ukv[:, :, :MLA_NOPE], ((0, 0), (0, 0), (0, LANES - MLA_NOPE)))
    wv = w_ukv[:, :, MLA_NOPE:]
    wkv_cat = jnp.concatenate([wk_pad.reshape(kv_rank, n_heads * LANES),
                               wv.reshape(kv_rank, n_heads * MLA_V)], axis=1).astype(BF16)
    return win_ext, wq_pad, wkv_cat


def kernel(x, positions, attn_pre_norm, w_in, mla_q_norm, mla_w_uq, mla_kv_norm, mla_w_ukv, mla_out_norm,
           hgrn_lb_logits, hgrn_out_norm, w_out, attn_post_norm, ffn_pre_norm, w_gate, w_up, w_down,
           ffn_post_norm):
    b, s, d = x.shape
    assert attn_pre_norm.shape[0] == 1, "single-layer block"
    assert mla_w_uq.shape[3] == MLA_NOPE + MLA_ROPE and 2 * MLA_ROPE + MLA_NOPE == LANES
    q_rank = mla_q_norm.shape[-1]
    kv_rank = mla_kv_norm.shape[-1]
    n_heads = mla_w_uq.shape[2]
    hw = hgrn_out_norm.shape[-1]
    row = lambda a: a.reshape(1, -1)

    cos_t, sin_t = _rope_tables(positions)
    win_ext, wq_pad, wkv_cat = _prep_in_proj_weights(w_in[0], mla_w_uq[0], mla_w_ukv[0],
                                                     q_rank=q_rank, kv_rank=kv_rank)
    qt, k, vt, hq, lf, kk, hi, g = _in_proj(
        x, cos_t, sin_t, row(attn_pre_norm[0]), win_ext, row(mla_q_norm[0]), wq_pad,
        row(mla_kv_norm[0]), wkv_cat, hgrn_lb_logits, n_heads=n_heads, hw=hw, tm=512, n_sub=2)
    o_mla = _mla_attention(qt, k, vt, row(mla_out_norm[0]), tq=512)
    o_hgrn = _hgrn2(hq, kk, hi, lf, g, row(hgrn_out_norm[0]), heads_per_step=2)

    out = _out_ffn(o_mla.reshape(b * s, -1), o_hgrn.reshape(b * s, -1), x.reshape(b * s, d),
                   w_out[0].astype(BF16), row(attn_post_norm[0]), row(ffn_pre_norm[0]),
                   w_gate[0].astype(BF16), w_up[0].astype(BF16), w_down[0].astype(BF16),
                   row(ffn_post_norm[0]), tm=1024, n_sub=4)
    return out.reshape(b, s, d)
```

```python
import functools
import math

import numpy as np
import jax
import jax.numpy as jnp
from jax import lax
from jax.experimental import pallas as pl
from jax.experimental.pallas import tpu as pltpu

F32 = jnp.float32
BF16 = jnp.bfloat16

EPS = 1e-6
ROPE_THETA = 10000.0
LOG2E = math.log2(math.e)
LANES = 128
BF16_SUBLANES = 16
MLA_NOPE = 64
MLA_ROPE = 32
MLA_V = 64
ONES_ROWS = BF16_SUBLANES
F32_SUBLANES = 8
HGRN_HEAD = 128
HGRN_BLOCK = 256
HGRN_CUMSUM_GROUP = 64
BLOCKS_PER_TRIP = 4
NEG_BIG = -1e30
MIB = 1024 * 1024


def _rms(x, w):
    inv = lax.rsqrt(jnp.mean(x * x, axis=-1, keepdims=True) + EPS)
    return (x * inv) * w


def _sigmoid(x):
    return 1.0 / (1.0 + jnp.exp(-x))


def _dot(a, b):
    return jnp.dot(a, b, preferred_element_type=F32)


def _dot_nt(a, b):
    return lax.dot_general(a, b, (((1,), (1,)), ((), ())), preferred_element_type=F32)


def _dot_tn(a, b):
    return lax.dot_general(a, b, (((0,), (0,)), ((), ())), preferred_element_type=F32)


def _rope_kernel(pos_ref, invf_ref, cos_ref, sin_ref):
    ang = pos_ref[...].astype(F32) * invf_ref[...]
    cos_ref[...] = jnp.cos(ang)
    sin_ref[...] = jnp.sin(ang)


def _rope_tables(positions):
    n_tok = positions.size
    half = MLA_ROPE // 2
    inv_freq = 1.0 / (ROPE_THETA ** (jnp.arange(0, MLA_ROPE, 2, dtype=F32) / MLA_ROPE))
    blk = min(n_tok, 4096)
    spec = pl.BlockSpec((half, blk), lambda i: (0, i))
    shp = jax.ShapeDtypeStruct((half, n_tok), F32)
    return pl.pallas_call(
        _rope_kernel,
        grid=(n_tok // blk,),
        in_specs=[pl.BlockSpec((1, blk), lambda i: (0, i)), pl.BlockSpec((half, 1), lambda i: (0, 0))],
        out_specs=[spec, spec],
        out_shape=[shp, shp],
        name="rope_tables",
    )(positions.reshape(1, n_tok), inv_freq.reshape(half, 1))


def _inproj_kernel(x_ref, ct_ref, st_ref, prew_ref, win_ref, qnw_ref, wq_ref,
                   kvnw_ref, wkv_ref, lbl_ref,
                   qt_out, k_out, vt_out, hq_out, lf_out, kk_out, hi_out, g_out,
                   *, qscale, n_heads, q_rank, kv_rank, hw, n_sub):
    sub = x_ref.shape[1] // n_sub
    tiles = [slice(i * sub, (i + 1) * sub) for i in range(n_sub)]
    r0, r1 = MLA_NOPE, MLA_NOPE + MLA_ROPE
    hp = n_heads * LANES
    c0 = q_rank + kv_rank + LANES

    lbl = lbl_ref[...]
    e = jnp.exp(lbl - jnp.max(lbl, axis=0, keepdims=True))
    lb = e[0:1] / jnp.sum(e, axis=0, keepdims=True)

    us = [_rms(x_ref[0, rows, :], prew_ref[...]).astype(BF16) for rows in tiles]
    lows = [_dot(u, win_ref[:, :c0]) for u in us]

    def gate_group(g, u, rows):
        y = _dot(u, win_ref[:, c0 + g * hw:c0 + (g + 1) * hw])
        if g == 0:
            hq_out[0, rows, :] = (y * _sigmoid(y)).astype(BF16)
        elif g == 1:
            f = lb + (1.0 - lb) * _sigmoid(y)
            lf_out[0, rows, :] = jnp.log(f) * LOG2E
            kk_out[0, rows, :] = (1.0 - f).astype(BF16)
        elif g == 2:
            hi_out[0, rows, :] = y.astype(BF16)
        else:
            g_out[0, rows, :] = (y * _sigmoid(y)).astype(BF16)

    def mla_group(low, rows):
        c_q = low[:, :q_rank]
        c_kv = low[:, q_rank:q_rank + kv_rank]
        kr = low[:, q_rank + kv_rank:]
        ct = ct_ref[:, rows]
        st = st_ref[:, rows]
        cc = jnp.concatenate([ct, ct], axis=0)
        ss = jnp.concatenate([-st, st], axis=0)

        cqn = _rms(c_q, qnw_ref[...]).astype(BF16)
        qq = _dot(cqn, wq_ref[...])
        ccq = cc * qscale
        ssq = ss * qscale
        zpad = jnp.zeros((LANES - r1, sub), F32)
        for h in range(n_heads):
            xt = qq[:, h * LANES:(h + 1) * LANES].T
            rope = xt[r0:r1] * ccq + xt[r1:] * ssq
            qt_out[0, h, :, rows] = jnp.concatenate([xt[:r0] * qscale, rope, zpad], axis=0).astype(BF16)

        z64 = jnp.zeros((r0, sub), F32)
        ck = jnp.concatenate([z64, cc, zpad], axis=0).T
        sk = jnp.concatenate([z64, zpad, ss], axis=0).T
        k_rope = kr * ck + pltpu.roll(kr * sk, LANES - MLA_ROPE, 1)

        ckvn = _rms(c_kv, kvnw_ref[...]).astype(BF16)
        kvv = _dot(ckvn, wkv_ref[...])
        for h in range(n_heads):
            k_out[0, h, rows, :] = (kvv[:, h * LANES:(h + 1) * LANES] + k_rope).astype(BF16)
        vt_out[0, :, rows] = kvv[:, hp:].T.astype(BF16)

    for u, low, rows in zip(us, lows, tiles):
        gate_group(0, u, rows)
        mla_group(low, rows)
    for g in (1, 2, 3):
        for u, rows in zip(us, tiles):
            gate_group(g, u, rows)


def _in_proj(x, cos_t, sin_t, prew, win_ext, qnw, wq_pad, kvnw, wkv_cat, lbl, *, n_heads, hw, tm, n_sub):
    b, s, d = x.shape
    q_rank = qnw.shape[-1]
    kv_rank = kvnw.shape[-1]
    ns = s // tm
    qscale = (MLA_NOPE + MLA_ROPE) ** -0.5 * LOG2E
    kern = functools.partial(_inproj_kernel, qscale=qscale, n_heads=n_heads, q_rank=q_rank,
                             kv_rank=kv_rank, hw=hw, n_sub=n_sub)

    def full(a):
        return pl.BlockSpec(a.shape, lambda i, j: (0,) * a.ndim)

    tok = lambda w: pl.BlockSpec((1, tm, w), lambda i, j: (i, j, 0))
    head = pl.BlockSpec((1, n_heads, tm, LANES), lambda i, j: (i, 0, j, 0))
    head_t = pl.BlockSpec((1, n_heads, LANES, tm), lambda i, j: (i, 0, 0, j))
    tab_t = pl.BlockSpec((cos_t.shape[0], tm), lambda i, j: (0, i * ns + j))
    vw = n_heads * MLA_V
    out_shape = [
        jax.ShapeDtypeStruct((b, n_heads, LANES, s), BF16),
        jax.ShapeDtypeStruct((b, n_heads, s, LANES), BF16),
        jax.ShapeDtypeStruct((b, vw, s), BF16),
        jax.ShapeDtypeStruct((b, s, hw), BF16),
        jax.ShapeDtypeStruct((b, s, hw), F32),
        jax.ShapeDtypeStruct((b, s, hw), BF16),
        jax.ShapeDtypeStruct((b, s, hw), BF16),
        jax.ShapeDtypeStruct((b, s, hw), BF16),
    ]
    return pl.pallas_call(
        kern,
        grid=(b, ns),
        in_specs=[tok(d), tab_t, tab_t, full(prew), full(win_ext), full(qnw), full(wq_pad),
                  full(kvnw), full(wkv_cat), full(lbl)],
        out_specs=[head_t, head, pl.BlockSpec((1, vw, tm), lambda i, j: (i, 0, j)),
                   tok(hw), tok(hw), tok(hw), tok(hw), tok(hw)],
        out_shape=out_shape,
        compiler_params=pltpu.CompilerParams(
            dimension_semantics=("parallel", "parallel"), vmem_limit_bytes=56 * MIB),
        name="in_proj",
    )(x, cos_t, sin_t, prew, win_ext, qnw, wq_pad, kvnw, wkv_cat, lbl)


def _attn_kernel(qt_ref, k_ref, vt_ref, w_ref, o_ref, sa_ref, sb_ref, *, tq, n_q):
    kc = tq // 2
    ones = jnp.ones((ONES_ROWS, kc), BF16)
    nu = range(2)

    def scores(u, ks, s_ref, q_start):
        s_t = _dot(k_ref[0, u, pl.ds(ks, kc), :], qt_ref[0, u, :, pl.ds(q_start, tq)])
        s_ref[u] = s_t
        return jnp.max(s_t, axis=0, keepdims=True)

    def scores_right_half(u, ks, s_ref, q_start):
        s_ref[u, :, kc:] = _dot(k_ref[0, u, pl.ds(ks, kc), :], qt_ref[0, u, :, pl.ds(q_start + kc, kc)])

    def q_tile(qi, cm_first):
        q0 = pl.multiple_of(qi * tq, tq)

        def values(u, ks):
            return jnp.concatenate([vt_ref[0, u * MLA_V:(u + 1) * MLA_V, pl.ds(ks, kc)], ones], axis=0)

        def causal(s_sq):
            key = lax.broadcasted_iota(jnp.int32, s_sq.shape, 0)
            qry = lax.broadcasted_iota(jnp.int32, s_sq.shape, 1)
            return jnp.where(key <= qry, s_sq, -jnp.inf)

        def update(s_t, cmax, m, acc, v_ext):
            m_new = jnp.maximum(m, cmax)
            p_t = jnp.exp2(s_t - m_new).astype(BF16)
            return m_new, jnp.exp2(m - m_new) * acc + _dot(v_ext, p_t)

        def accumulate(u, ks, s_ref, cmax, m, acc):
            return update(s_ref[u], cmax, m, acc, values(u, ks))

        def accumulate_first_diagonal(u, s_ref, cmax, m, acc):
            s_t = s_ref[u]
            left = causal(s_t[:, :kc])
            s_t = jnp.concatenate([left, s_t[:, kc:]], axis=1)
            cmax = jnp.concatenate([jnp.max(left, axis=0, keepdims=True), cmax[:, kc:]], axis=1)
            return update(s_t, cmax, m, acc, values(u, q0))

        def accumulate_second_diagonal(u, s_ref, m, acc):
            s_t = causal(s_ref[u, :, kc:])
            m_r, acc_r = update(s_t, jnp.max(s_t, axis=0, keepdims=True), m[:, kc:], acc[:, kc:],
                                values(u, q0 + kc))
            return (jnp.concatenate([m[:, :kc], m_r], axis=1),
                    jnp.concatenate([acc[:, :kc], acc_r], axis=1))

        def pair_body(t, carry):
            k0 = pl.multiple_of(t * tq, tq)
            cm_b = [scores(u, k0 + kc, sb_ref, q0) for u in nu]
            st = [accumulate(u, k0, sa_ref, carry[u][0], carry[u][1], carry[u][2]) for u in nu]
            cm_a = [scores(u, k0 + tq, sa_ref, q0) for u in nu]
            st = [accumulate(u, k0 + kc, sb_ref, cm_b[u], st[u][0], st[u][1]) for u in nu]
            return tuple((cm_a[u], st[u][0], st[u][1]) for u in nu)

        init = tuple((cm_first[u], jnp.full((1, tq), NEG_BIG, F32),
                      jnp.zeros((MLA_V + ONES_ROWS, tq), F32)) for u in nu)
        carry = lax.fori_loop(0, qi // 2, lambda t, c: pair_body(2 * t + 1, pair_body(2 * t, c)), init)
        carry = lax.fori_loop(qi - qi % 2, qi, pair_body, carry)

        for u in nu:
            scores_right_half(u, q0 + kc, sb_ref, q0)
        st = [accumulate_first_diagonal(u, sa_ref, carry[u][0], carry[u][1], carry[u][2]) for u in nu]
        q_next = pl.multiple_of(jnp.minimum(qi + 1, n_q - 1) * tq, tq)
        cm_next = [scores(u, 0, sa_ref, q_next) for u in nu[:-1]]
        st = [accumulate_second_diagonal(u, sb_ref, st[u][0], st[u][1]) for u in nu]
        cm_next = tuple(cm_next + [scores(nu[-1], 0, sa_ref, q_next)])
        outs = []
        for u in nu:
            acc = st[u][1]
            o = acc[:MLA_V] / acc[MLA_V:MLA_V + 1]
            outs.append(o * lax.rsqrt(jnp.mean(o * o, axis=0, keepdims=True) + EPS))
        pair = jnp.concatenate(outs, axis=0).T
        o_ref[0, pl.ds(q0, tq), :] = (pair * w_ref[...]).astype(BF16)
        return cm_next

    lax.fori_loop(0, n_q, q_tile, tuple(scores(u, 0, sa_ref, 0) for u in nu))


def _mla_attention(qt, k, vt, w_norm, *, tq):
    b, n_heads, _, s = qt.shape
    kern = functools.partial(_attn_kernel, tq=tq, n_q=s // tq)
    s_scratch = pltpu.VMEM((2, tq // 2, tq), F32)
    return pl.pallas_call(
        kern,
        grid=(b, n_heads // 2),
        in_specs=[
            pl.BlockSpec((1, 2, LANES, s), lambda i, j: (i, j, 0, 0)),
            pl.BlockSpec((1, 2, s, LANES), lambda i, j: (i, j, 0, 0)),
            pl.BlockSpec((1, 2 * MLA_V, s), lambda i, j: (i, j, 0)),
            pl.BlockSpec((1, LANES), lambda i, j: (0, j)),
        ],
        out_specs=pl.BlockSpec((1, s, LANES), lambda i, j: (i, 0, j)),
        out_shape=jax.ShapeDtypeStruct((b, s, n_heads * MLA_V), BF16),
        scratch_shapes=[s_scratch, s_scratch],
        compiler_params=pltpu.CompilerParams(
            dimension_semantics=("parallel", "parallel"), vmem_limit_bytes=48 * MIB),
        name="mla_attn",
    )(qt, k, vt, w_norm)


def _hgrn_tables(c, g):
    n_lev = int(np.log2(c))
    t = np.arange(c)[:, None]
    s = np.arange(c)[None, :]
    lev = np.full((c, c), n_lev + 1, np.int32)
    for l in range(n_lev):
        m = c >> l
        half = m // 2
        lev[(t // m == s // m) & (t % m >= half) & (s % m < half)] = l
    lev[t == s] = n_lev
    hc = c // 2
    assert (lev[:hc, :hc] == lev[hc:, hc:]).all() and (lev[hc:, :hc] == 0).all()
    tri = np.tril(np.ones((g, g), np.float32))
    return np.concatenate([tri, tri, tri], axis=1), lev[:hc, :hc], n_lev


def _hgrn_level_exponents(b, lf, c):
    sub = lax.broadcasted_iota(jnp.int32, (F32_SUBLANES, LANES), 0)
    out = []
    m = c
    while m >= 2:
        half = m // 2
        if half >= F32_SUBLANES:
            pieces = []
            for lo in range(0, c, m):
                mid = lo + half
                r = b[mid - 1:mid]
                pieces += [r - b[lo:mid], b[mid:mid + half] - r]
            e = jnp.concatenate(pieces, axis=0)
        elif m == 2:
            odd = lax.broadcasted_iota(jnp.int32, lf.shape, 0) % 2 == 1
            e = jnp.where(odd, lf, 0.0)
        else:
            sign = jnp.where(sub % m >= half, 1.0, -1.0)
            pieces = []
            for v0 in range(0, c, F32_SUBLANES):
                ref = b[v0 + half - 1:v0 + half]
                for j in range(1, F32_SUBLANES // m):
                    ref = jnp.where(sub >= j * m, b[v0 + j * m + half - 1:v0 + j * m + half], ref)
                pieces.append(sign * (b[v0:v0 + F32_SUBLANES] - ref))
            e = jnp.concatenate(pieces, axis=0)
        out.append((m, e))
        m = half
    return out


def _hgrn_kernel(q_ref, k_ref, v_ref, lf_ref, g_ref, w_ref, tri_ref, lev_ref, o_ref,
                 *, c, g, n_blocks, n_hd, n_lev):
    tri = tri_ref[...]
    ng = c // g
    hc = c // 2

    def where(i, hd):
        return pl.ds(pl.multiple_of(i * c, c), c), slice(hd * HGRN_HEAD, (hd + 1) * HGRN_HEAD)

    def cumulative_gates(i, hd):
        rows, cols = where(i, hd)
        lf = lf_ref[0, rows, cols]
        lf_w = jnp.concatenate([lf[j * g:(j + 1) * g] for j in range(ng)], axis=1)
        hi = lf_w.astype(BF16)
        r1 = lf_w - hi.astype(F32)
        mid = r1.astype(BF16)
        lo = (r1 - mid.astype(F32)).astype(BF16)
        b_w = _dot(tri, jnp.concatenate([hi, mid, lo], axis=0))
        parts = []
        off = None
        for j in range(ng):
            bj = b_w[:, j * LANES:(j + 1) * LANES]
            if off is not None:
                bj = bj + off
            parts.append(bj)
            off = bj[g - 1:g]
        return lf, jnp.concatenate(parts, axis=0), off

    def intra_weights(i, hd, lf, b, b_end):
        rows, cols = where(i, hd)
        qb = q_ref[0, rows, cols]
        kb = k_ref[0, rows, cols]
        q = qb.astype(F32)
        k = kb.astype(F32)
        lev = lev_ref[...]
        halves = (slice(0, hc), slice(hc, c))
        a_diag = [jnp.zeros((hc, hc), F32), jnp.zeros((hc, hc), F32)]
        a10 = None
        row = lax.broadcasted_iota(jnp.int32, q.shape, 0)
        for l, (m, e) in enumerate(_hgrn_level_exponents(b, lf, c)):
            half = m // 2
            if half >= F32_SUBLANES:
                mixed = jnp.concatenate(
                    [piece for lo in range(0, c, m) for piece in (k[lo:lo + half], q[lo + half:lo + m])],
                    axis=0)
            else:
                mixed = jnp.where(row % m >= half, q, k)
            r = (mixed * jnp.exp2(e)).astype(BF16)
            if l == 0:
                a10 = _dot_nt(r[hc:], r[:hc])
                continue
            for j, rs in enumerate(halves):
                prod = _dot_nt(r[rs], r[rs])
                if half >= F32_SUBLANES:
                    pieces = []
                    for lo in range(0, hc, m):
                        low = slice(lo + half, lo + m)
                        pieces += [a_diag[j][lo:lo + half],
                                   jnp.where(lev[low] == l, prod[low], a_diag[j][low])]
                    a_diag[j] = jnp.concatenate(pieces, axis=0)
                else:
                    a_diag[j] = jnp.where(lev == l, prod, a_diag[j])
        here = lev == n_lev
        for j, rs in enumerate(halves):
            a_diag[j] = jnp.where(here, _dot_nt(qb[rs], kb[rs]), a_diag[j])
        a = jnp.concatenate(
            [jnp.concatenate([a_diag[0], jnp.zeros((hc, hc), F32)], axis=1),
             jnp.concatenate([a10, a_diag[1]], axis=1)], axis=0)
        q_in = (q * jnp.exp2(b)).astype(BF16)
        k_out = (k * jnp.exp2(b_end - b)).astype(BF16)
        return a.astype(BF16), q_in, k_out

    def read_out(i, hd, b_end, a, q_in, k_out, st):
        rows, cols = where(i, hd)
        v = v_ref[0, rows, cols]
        o = _dot_nt(q_in, st.astype(BF16)) + _dot(a, v)
        o = _rms(o, w_ref[:, cols]) * g_ref[0, rows, cols].astype(F32)
        o_ref[0, rows, cols] = o.astype(BF16)
        return st * jnp.exp2(b_end) + _dot_tn(v, k_out)

    init = tuple(jnp.zeros((HGRN_HEAD, HGRN_HEAD), F32) for _ in range(n_hd))

    def trip(t, states):
        units = [(BLOCKS_PER_TRIP * t + j, hd) for j in range(BLOCKS_PER_TRIP) for hd in range(n_hd)]
        gates = [cumulative_gates(i, hd) for i, hd in units]
        states = list(states)
        pending = None
        for u, (i, hd) in enumerate(units):
            lf, b, b_end = gates[u]
            ready = (i, hd, b_end) + intra_weights(i, hd, lf, b, b_end)
            if pending is not None:
                states[pending[1]] = read_out(*pending, states[pending[1]])
            pending = ready
        states[pending[1]] = read_out(*pending, states[pending[1]])
        return tuple(states)

    lax.fori_loop(0, n_blocks // BLOCKS_PER_TRIP, trip, init)


def _hgrn2(hq, kk, hi, lf, g, w_norm, *, heads_per_step):
    b, s, hw = hq.shape
    c = HGRN_BLOCK
    width = heads_per_step * HGRN_HEAD
    tri_np, lev_np, n_lev = _hgrn_tables(c, HGRN_CUMSUM_GROUP)
    tri = jnp.asarray(tri_np, BF16)
    lev = jnp.asarray(lev_np)
    kern = functools.partial(_hgrn_kernel, c=c, g=HGRN_CUMSUM_GROUP, n_blocks=s // c,
                             n_hd=heads_per_step, n_lev=n_lev)
    tok = pl.BlockSpec((1, s, width), lambda i, j: (i, 0, j))
    return pl.pallas_call(
        kern,
        grid=(b, hw // width),
        in_specs=[tok, tok, tok, tok, tok,
                  pl.BlockSpec((1, width), lambda i, j: (0, j)),
                  pl.BlockSpec(tri.shape, lambda i, j: (0, 0)),
                  pl.BlockSpec(lev.shape, lambda i, j: (0, 0))],
        out_specs=tok,
        out_shape=jax.ShapeDtypeStruct((b, s, hw), BF16),
        compiler_params=pltpu.CompilerParams(
            dimension_semantics=("parallel", "parallel"), vmem_limit_bytes=48 * MIB),
        name="hgrn2",
    )(hq, kk, hi, lf, g, w_norm, tri, lev)


def _out_ffn_kernel(om_ref, oh_ref, x_ref, wo_ref, postw_ref, prew_ref, wg_ref, wu_ref, wd_ref,
                    fpostw_ref, out_ref, *, n_sub):
    sub = x_ref.shape[0] // n_sub
    tiles = [slice(i * sub, (i + 1) * sub) for i in range(n_sub)]
    hs = []
    for rows in tiles:
        mix = jnp.concatenate([om_ref[rows, :], oh_ref[rows, :]], axis=-1)
        hs.append(x_ref[rows, :] + _rms(_dot(mix, wo_ref[...]), postw_ref[...]))
    ffs = []
    for h in hs:
        z = _rms(h, prew_ref[...]).astype(BF16)
        gate = _dot(z, wg_ref[...])
        ffs.append((gate * _sigmoid(gate) * _dot(z, wu_ref[...])).astype(BF16))
    for rows, h, ff in zip(tiles, hs, ffs):
        out_ref[rows, :] = h + _rms(_dot(ff, wd_ref[...]), fpostw_ref[...])


def _out_ffn(om, oh, x, wo, postw, prew, wg, wu, wd, fpostw, *, tm, n_sub):
    n_tok, d = x.shape
    kern = functools.partial(_out_ffn_kernel, n_sub=n_sub)

    def full(a):
        return pl.BlockSpec(a.shape, lambda i: (0,) * a.ndim, pipeline_mode=pl.Buffered(1))

    tok = lambda w: pl.BlockSpec((tm, w), lambda i: (i, 0))
    return pl.pallas_call(
        kern,
        grid=(n_tok // tm,),
        in_specs=[tok(om.shape[1]), tok(oh.shape[1]), tok(d), full(wo), full(postw), full(prew),
                  full(wg), full(wu), full(wd), full(fpostw)],
        out_specs=tok(d),
        out_shape=jax.ShapeDtypeStruct((n_tok, d), F32),
        compiler_params=pltpu.CompilerParams(
            dimension_semantics=("parallel",), vmem_limit_bytes=56 * MIB),
        name="out_ffn",
    )(om, oh, x, wo, postw, prew, wg, wu, wd, fpostw)


def _prep_in_proj_weights(w_in, w_uq, w_ukv, *, q_rank, kv_rank):
    d = w_in.shape[0]
    half = MLA_ROPE // 2
    s2 = q_rank + kv_rank
    s3 = s2 + MLA_ROPE
    kr = w_in[:, s2:s3]
    z_lo = jnp.zeros((d, MLA_NOPE), w_in.dtype)
    kr_slab = jnp.concatenate([z_lo, kr, kr[:, half:], kr[:, :half]], axis=1)
    win_ext = jnp.concatenate([w_in[:, :s2], kr_slab, w_in[:, s3:]], axis=1).astype(BF16)

    n_heads = w_uq.shape[1]
    rope = w_uq[:, :, MLA_NOPE:]
    wq_pad = jnp.concatenate([w_uq, rope[:, :, half:], rope[:, :, :half]], axis=2)
    wq_pad = wq_pad.reshape(q_rank, n_heads * LANES).astype(BF16)

    wk_pad = jnp.pad(w_ukv[:, :, :MLA_NOPE], ((0, 0), (0, 0), (0, LANES - MLA_NOPE)))
    wv = w_ukv[:, :, MLA_NOPE:]
    wkv_cat = jnp.concatenate([wk_pad.reshape(kv_rank, n_heads * LANES),
                               wv.reshape(kv_rank, n_heads * MLA_V)], axis=1).astype(BF16)
    return win_ext, wq_pad, wkv_cat


def kernel(x, positions, attn_pre_norm, w_in, mla_q_norm, mla_w_uq, mla_kv_norm, mla_w_ukv, mla_out_norm,
           hgrn_lb_logits, hgrn_out_norm, w_out, attn_post_norm, ffn_pre_norm, w_gate, w_up, w_down,
           ffn_post_norm):
    b, s, d = x.shape
    assert attn_pre_norm.shape[0] == 1, "single-layer block"
    assert mla_w_uq.shape[3] == MLA_NOPE + MLA_ROPE and 2 * MLA_ROPE + MLA_NOPE == LANES
    q_rank = mla_q_norm.shape[-1]
    kv_rank = mla_kv_norm.shape[-1]
    n_heads = mla_w_uq.shape[2]
    hw = hgrn_out_norm.shape[-1]
    row = lambda a: a.reshape(1, -1)

    cos_t, sin_t = _rope_tables(positions)
    win_ext, wq_pad, wkv_cat = _prep_in_proj_weights(w_in[0], mla_w_uq[0], mla_w_ukv[0],
                                                     q_rank=q_rank, kv_rank=kv_rank)
    qt, k, vt, hq, lf, kk, hi, g = _in_proj(
        x, cos_t, sin_t, row(attn_pre_norm[0]), win_ext, row(mla_q_norm[0]), wq_pad,
        row(mla_kv_norm[0]), wkv_cat, hgrn_lb_logits, n_heads=n_heads, hw=hw, tm=1024, n_sub=4)
    o_mla = _mla_attention(qt, k, vt, row(mla_out_norm[0]), tq=512)
    o_hgrn = _hgrn2(hq, kk, hi, lf, g, row(hgrn_out_norm[0]), heads_per_step=2)

    out = _out_ffn(o_mla.reshape(b * s, -1), o_hgrn.reshape(b * s, -1), x.reshape(b * s, d),
                   w_out[0].astype(BF16), row(attn_post_norm[0]), row(ffn_pre_norm[0]),
                   w_gate[0].astype(BF16), w_up[0].astype(BF16), w_down[0].astype(BF16),
                   row(ffn_post_norm[0]), tm=1024, n_sub=4)
    return out.reshape(b, s, d)
```

```python
import functools
import math

import numpy as np
import jax
import jax.numpy as jnp
from jax import lax
from jax.experimental import pallas as pl
from jax.experimental.pallas import tpu as pltpu

F32 = jnp.float32
BF16 = jnp.bfloat16

EPS = 1e-6
ROPE_THETA = 10000.0
LOG2E = math.log2(math.e)
LANES = 128
BF16_SUBLANES = 16
MLA_NOPE = 64
MLA_ROPE = 32
MLA_V = 64
ONES_ROWS = BF16_SUBLANES
F32_SUBLANES = 8
HGRN_HEAD = 128
HGRN_BLOCK = 256
HGRN_CUMSUM_GROUP = 64
BLOCKS_PER_TRIP = 4
NEG_BIG = -1e30
MIB = 1024 * 1024
V7X_VMEM_BYTES = 64 * MIB


def _nbytes(shape, dtype):
    return math.prod(shape) * jnp.dtype(dtype).itemsize


def _vmem_limit(pipelined, resident=(), temporaries=0):
    total = 2 * sum(_nbytes(s, d) for s, d in pipelined) + sum(_nbytes(s, d) for s, d in resident)
    return min(total + temporaries, V7X_VMEM_BYTES)


def _rms(x, w):
    inv = lax.rsqrt(jnp.mean(x * x, axis=-1, keepdims=True) + EPS)
    return (x * inv) * w


def _sigmoid(x):
    return 1.0 / (1.0 + jnp.exp(-x))


def _dot(a, b):
    return jnp.dot(a, b, preferred_element_type=F32)


def _dot_nt(a, b):
    return lax.dot_general(a, b, (((1,), (1,)), ((), ())), preferred_element_type=F32)


def _dot_tn(a, b):
    return lax.dot_general(a, b, (((0,), (0,)), ((), ())), preferred_element_type=F32)


def _rope_kernel(pos_ref, invf_ref, cos_ref, sin_ref):
    ang = pos_ref[...].astype(F32) * invf_ref[...]
    cos_ref[...] = jnp.cos(ang)
    sin_ref[...] = jnp.sin(ang)


def _rope_tables(positions):
    n_tok = positions.size
    half = MLA_ROPE // 2
    inv_freq = 1.0 / (ROPE_THETA ** (jnp.arange(0, MLA_ROPE, 2, dtype=F32) / MLA_ROPE))
    blk = min(n_tok, 4096)
    spec = pl.BlockSpec((half, blk), lambda i: (0, i))
    shp = jax.ShapeDtypeStruct((half, n_tok), F32)
    return pl.pallas_call(
        _rope_kernel,
        grid=(n_tok // blk,),
        in_specs=[pl.BlockSpec((1, blk), lambda i: (0, i)), pl.BlockSpec((half, 1), lambda i: (0, 0))],
        out_specs=[spec, spec],
        out_shape=[shp, shp],
        name="rope_tables",
    )(positions.reshape(1, n_tok), inv_freq.reshape(half, 1))


def _inproj_kernel(x_ref, ct_ref, st_ref, prew_ref, win_ref, qnw_ref, wq_ref,
                   kvnw_ref, wkv_ref, lbl_ref,
                   qt_out, k_out, vt_out, hq_out, lf_out, kk_out, hi_out, g_out,
                   *, qscale, n_heads, q_rank, kv_rank, hw, n_sub):
    sub = x_ref.shape[1] // n_sub
    tiles = [slice(i * sub, (i + 1) * sub) for i in range(n_sub)]
    r0, r1 = MLA_NOPE, MLA_NOPE + MLA_ROPE
    hp = n_heads * LANES
    c0 = q_rank + kv_rank + LANES

    lbl = lbl_ref[...]
    e = jnp.exp(lbl - jnp.max(lbl, axis=0, keepdims=True))
    lb = e[0:1] / jnp.sum(e, axis=0, keepdims=True)

    us = [_rms(x_ref[0, rows, :], prew_ref[...]).astype(BF16) for rows in tiles]
    lows = [_dot(u, win_ref[:, :c0]) for u in us]

    def gate_group(g, u, rows):
        y = _dot(u, win_ref[:, c0 + g * hw:c0 + (g + 1) * hw])
        if g == 0:
            hq_out[0, rows, :] = (y * _sigmoid(y)).astype(BF16)
        elif g == 1:
            f = lb + (1.0 - lb) * _sigmoid(y)
            lf_out[0, rows, :] = jnp.log(f) * LOG2E
            kk_out[0, rows, :] = (1.0 - f).astype(BF16)
        elif g == 2:
            hi_out[0, rows, :] = y.astype(BF16)
        else:
            g_out[0, rows, :] = (y * _sigmoid(y)).astype(BF16)

    def mla_group(low, rows):
        c_q = low[:, :q_rank]
        c_kv = low[:, q_rank:q_rank + kv_rank]
        kr = low[:, q_rank + kv_rank:]
        ct = ct_ref[:, rows]
        st = st_ref[:, rows]
        cc = jnp.concatenate([ct, ct], axis=0)
        ss = jnp.concatenate([-st, st], axis=0)

        cqn = _rms(c_q, qnw_ref[...]).astype(BF16)
        qq = _dot(cqn, wq_ref[...])
        ccq = cc * qscale
        ssq = ss * qscale
        zpad = jnp.zeros((LANES - r1, sub), F32)
        for h in range(n_heads):
            xt = qq[:, h * LANES:(h + 1) * LANES].T
            rope = xt[r0:r1] * ccq + xt[r1:] * ssq
            qt_out[0, h, :, rows] = jnp.concatenate([xt[:r0] * qscale, rope, zpad], axis=0).astype(BF16)

        z64 = jnp.zeros((r0, sub), F32)
        ck = jnp.concatenate([z64, cc, zpad], axis=0).T
        sk = jnp.concatenate([z64, zpad, ss], axis=0).T
        k_rope = kr * ck + pltpu.roll(kr * sk, LANES - MLA_ROPE, 1)

        ckvn = _rms(c_kv, kvnw_ref[...]).astype(BF16)
        kvv = _dot(ckvn, wkv_ref[...])
        for h in range(n_heads):
            k_out[0, h, rows, :] = (kvv[:, h * LANES:(h + 1) * LANES] + k_rope).astype(BF16)
        vt_out[0, :, rows] = kvv[:, hp:].T.astype(BF16)

    for u, low, rows in zip(us, lows, tiles):
        gate_group(0, u, rows)
        mla_group(low, rows)
    for g in (1, 2, 3):
        for u, rows in zip(us, tiles):
            gate_group(g, u, rows)


def _in_proj(x, cos_t, sin_t, prew, win_ext, qnw, wq_pad, kvnw, wkv_cat, lbl, *, n_heads, hw, tm, n_sub):
    b, s, d = x.shape
    q_rank = qnw.shape[-1]
    kv_rank = kvnw.shape[-1]
    ns = s // tm
    qscale = (MLA_NOPE + MLA_ROPE) ** -0.5 * LOG2E
    kern = functools.partial(_inproj_kernel, qscale=qscale, n_heads=n_heads, q_rank=q_rank,
                             kv_rank=kv_rank, hw=hw, n_sub=n_sub)

    def full(a):
        return pl.BlockSpec(a.shape, lambda i, j: (0,) * a.ndim)

    tok = lambda w: pl.BlockSpec((1, tm, w), lambda i, j: (i, j, 0))
    head = pl.BlockSpec((1, n_heads, tm, LANES), lambda i, j: (i, 0, j, 0))
    head_t = pl.BlockSpec((1, n_heads, LANES, tm), lambda i, j: (i, 0, 0, j))
    tab_t = pl.BlockSpec((cos_t.shape[0], tm), lambda i, j: (0, i * ns + j))
    vw = n_heads * MLA_V
    out_shape = [
        jax.ShapeDtypeStruct((b, n_heads, LANES, s), BF16),
        jax.ShapeDtypeStruct((b, n_heads, s, LANES), BF16),
        jax.ShapeDtypeStruct((b, vw, s), BF16),
        jax.ShapeDtypeStruct((b, s, hw), BF16),
        jax.ShapeDtypeStruct((b, s, hw), F32),
        jax.ShapeDtypeStruct((b, s, hw), BF16),
        jax.ShapeDtypeStruct((b, s, hw), BF16),
        jax.ShapeDtypeStruct((b, s, hw), BF16),
    ]
    sub = tm // n_sub
    vmem = _vmem_limit(
        pipelined=[((tm, d), F32), ((2 * cos_t.shape[0], tm), F32), ((2, n_heads, tm, LANES), BF16),
                   ((vw, tm), BF16), ((tm, hw), F32), ((4, tm, hw), BF16)]
        + [(a.shape, a.dtype) for a in (win_ext, wq_pad, wkv_cat)],
        temporaries=2 * _nbytes((sub, win_ext.shape[1] - 3 * hw + wq_pad.shape[1] + wkv_cat.shape[1] + hw), F32))
    return pl.pallas_call(
        kern,
        grid=(b, ns),
        in_specs=[tok(d), tab_t, tab_t, full(prew), full(win_ext), full(qnw), full(wq_pad),
                  full(kvnw), full(wkv_cat), full(lbl)],
        out_specs=[head_t, head, pl.BlockSpec((1, vw, tm), lambda i, j: (i, 0, j)),
                   tok(hw), tok(hw), tok(hw), tok(hw), tok(hw)],
        out_shape=out_shape,
        compiler_params=pltpu.CompilerParams(
            dimension_semantics=("parallel", "parallel"), vmem_limit_bytes=vmem),
        name="in_proj",
    )(x, cos_t, sin_t, prew, win_ext, qnw, wq_pad, kvnw, wkv_cat, lbl)


def _attn_kernel(qt_ref, k_ref, vt_ref, w_ref, o_ref, sa_ref, sb_ref, *, tq, n_q):
    kc = tq // 2
    ones = jnp.ones((ONES_ROWS, kc), BF16)
    nu = range(2)

    def scores(u, ks, s_ref, q_start):
        s_t = _dot(k_ref[0, u, pl.ds(ks, kc), :], qt_ref[0, u, :, pl.ds(q_start, tq)])
        s_ref[u] = s_t
        return jnp.max(s_t, axis=0, keepdims=True)

    def scores_right_half(u, ks, s_ref, q_start):
        s_ref[u, :, kc:] = _dot(k_ref[0, u, pl.ds(ks, kc), :], qt_ref[0, u, :, pl.ds(q_start + kc, kc)])

    def q_tile(qi, cm_first):
        q0 = pl.multiple_of(qi * tq, tq)

        def values(u, ks):
            return jnp.concatenate([vt_ref[0, u * MLA_V:(u + 1) * MLA_V, pl.ds(ks, kc)], ones], axis=0)

        def causal(s_sq):
            key = lax.broadcasted_iota(jnp.int32, s_sq.shape, 0)
            qry = lax.broadcasted_iota(jnp.int32, s_sq.shape, 1)
            return jnp.where(key <= qry, s_sq, -jnp.inf)

        def update(s_t, cmax, m, acc, v_ext):
            m_new = jnp.maximum(m, cmax)
            p_t = jnp.exp2(s_t - m_new).astype(BF16)
            return m_new, jnp.exp2(m - m_new) * acc + _dot(v_ext, p_t)

        def accumulate(u, ks, s_ref, cmax, m, acc):
            return update(s_ref[u], cmax, m, acc, values(u, ks))

        def accumulate_first_diagonal(u, s_ref, cmax, m, acc):
            s_t = s_ref[u]
            left = causal(s_t[:, :kc])
            s_t = jnp.concatenate([left, s_t[:, kc:]], axis=1)
            cmax = jnp.concatenate([jnp.max(left, axis=0, keepdims=True), cmax[:, kc:]], axis=1)
            return update(s_t, cmax, m, acc, values(u, q0))

        def accumulate_second_diagonal(u, s_ref, m, acc):
            s_t = causal(s_ref[u, :, kc:])
            m_r, acc_r = update(s_t, jnp.max(s_t, axis=0, keepdims=True), m[:, kc:], acc[:, kc:],
                                values(u, q0 + kc))
            return (jnp.concatenate([m[:, :kc], m_r], axis=1),
                    jnp.concatenate([acc[:, :kc], acc_r], axis=1))

        def pair_body(t, carry):
            k0 = pl.multiple_of(t * tq, tq)
            cm_b = [scores(u, k0 + kc, sb_ref, q0) for u in nu]
            st = [accumulate(u, k0, sa_ref, carry[u][0], carry[u][1], carry[u][2]) for u in nu]
            cm_a = [scores(u, k0 + tq, sa_ref, q0) for u in nu]
            st = [accumulate(u, k0 + kc, sb_ref, cm_b[u], st[u][0], st[u][1]) for u in nu]
            return tuple((cm_a[u], st[u][0], st[u][1]) for u in nu)

        init = tuple((cm_first[u], jnp.full((1, tq), NEG_BIG, F32),
                      jnp.zeros((MLA_V + ONES_ROWS, tq), F32)) for u in nu)
        carry = lax.fori_loop(0, qi // 2, lambda t, c: pair_body(2 * t + 1, pair_body(2 * t, c)), init)
        carry = lax.fori_loop(qi - qi % 2, qi, pair_body, carry)

        for u in nu:
            scores_right_half(u, q0 + kc, sb_ref, q0)
        st = [accumulate_first_diagonal(u, sa_ref, carry[u][0], carry[u][1], carry[u][2]) for u in nu]
        q_next = pl.multiple_of(jnp.minimum(qi + 1, n_q - 1) * tq, tq)
        cm_next = [scores(u, 0, sa_ref, q_next) for u in nu[:-1]]
        st = [accumulate_second_diagonal(u, sb_ref, st[u][0], st[u][1]) for u in nu]
        cm_next = tuple(cm_next + [scores(nu[-1], 0, sa_ref, q_next)])
        outs = []
        for u in nu:
            acc = st[u][1]
            o = acc[:MLA_V] / acc[MLA_V:MLA_V + 1]
            outs.append(o * lax.rsqrt(jnp.mean(o * o, axis=0, keepdims=True) + EPS))
        pair = jnp.concatenate(outs, axis=0).T
        o_ref[0, pl.ds(q0, tq), :] = (pair * w_ref[...]).astype(BF16)
        return cm_next

    lax.fori_loop(0, n_q, q_tile, tuple(scores(u, 0, sa_ref, 0) for u in nu))


def _mla_attention(qt, k, vt, w_norm, *, tq):
    b, n_heads, _, s = qt.shape
    kern = functools.partial(_attn_kernel, tq=tq, n_q=s // tq)
    s_scratch = pltpu.VMEM((2, tq // 2, tq), F32)
    vmem = _vmem_limit(
        pipelined=[((2, LANES, s), BF16), ((2, s, LANES), BF16), ((2 * MLA_V, s), BF16), ((s, LANES), BF16)],
        resident=[(s_scratch.shape, F32)] * 2,
        temporaries=4 * (_nbytes((MLA_V + ONES_ROWS, tq), F32) + _nbytes((tq // 2, tq), F32)
                         + _nbytes((tq // 2, tq), BF16)))
    return pl.pallas_call(
        kern,
        grid=(b, n_heads // 2),
        in_specs=[
            pl.BlockSpec((1, 2, LANES, s), lambda i, j: (i, j, 0, 0)),
            pl.BlockSpec((1, 2, s, LANES), lambda i, j: (i, j, 0, 0)),
            pl.BlockSpec((1, 2 * MLA_V, s), lambda i, j: (i, j, 0)),
            pl.BlockSpec((1, LANES), lambda i, j: (0, j)),
        ],
        out_specs=pl.BlockSpec((1, s, LANES), lambda i, j: (i, 0, j)),
        out_shape=jax.ShapeDtypeStruct((b, s, n_heads * MLA_V), BF16),
        scratch_shapes=[s_scratch, s_scratch],
        compiler_params=pltpu.CompilerParams(
            dimension_semantics=("parallel", "parallel"), vmem_limit_bytes=vmem),
        name="mla_attn",
    )(qt, k, vt, w_norm)


def _hgrn_tables(c, g):
    n_lev = int(np.log2(c))
    t = np.arange(c)[:, None]
    s = np.arange(c)[None, :]
    lev = np.full((c, c), n_lev + 1, np.int32)
    for l in range(n_lev):
        m = c >> l
        half = m // 2
        lev[(t // m == s // m) & (t % m >= half) & (s % m < half)] = l
    lev[t == s] = n_lev
    hc = c // 2
    assert (lev[:hc, :hc] == lev[hc:, hc:]).all() and (lev[hc:, :hc] == 0).all()
    tri = np.tril(np.ones((g, g), np.float32))
    return np.concatenate([tri, tri, tri], axis=1), lev[:hc, :hc], n_lev


def _hgrn_level_exponents(b, lf, c):
    sub = lax.broadcasted_iota(jnp.int32, (F32_SUBLANES, LANES), 0)
    out = []
    m = c
    while m >= 2:
        half = m // 2
        if half >= F32_SUBLANES:
            pieces = []
            for lo in range(0, c, m):
                mid = lo + half
                r = b[mid - 1:mid]
                pieces += [r - b[lo:mid], b[mid:mid + half] - r]
            e = jnp.concatenate(pieces, axis=0)
        elif m == 2:
            odd = lax.broadcasted_iota(jnp.int32, lf.shape, 0) % 2 == 1
            e = jnp.where(odd, lf, 0.0)
        else:
            sign = jnp.where(sub % m >= half, 1.0, -1.0)
            pieces = []
            for v0 in range(0, c, F32_SUBLANES):
                ref = b[v0 + half - 1:v0 + half]
                for j in range(1, F32_SUBLANES // m):
                    ref = jnp.where(sub >= j * m, b[v0 + j * m + half - 1:v0 + j * m + half], ref)
                pieces.append(sign * (b[v0:v0 + F32_SUBLANES] - ref))
            e = jnp.concatenate(pieces, axis=0)
        out.append((m, e))
        m = half
    return out


def _hgrn_kernel(q_ref, k_ref, v_ref, lf_ref, g_ref, w_ref, tri_ref, lev_ref, o_ref,
                 *, c, g, n_blocks, n_hd, n_lev):
    tri = tri_ref[...]
    ng = c // g
    hc = c // 2

    def where(i, hd):
        return pl.ds(pl.multiple_of(i * c, c), c), slice(hd * HGRN_HEAD, (hd + 1) * HGRN_HEAD)

    def cumulative_gates(i, hd):
        rows, cols = where(i, hd)
        lf = lf_ref[0, rows, cols]
        lf_w = jnp.concatenate([lf[j * g:(j + 1) * g] for j in range(ng)], axis=1)
        hi = lf_w.astype(BF16)
        r1 = lf_w - hi.astype(F32)
        mid = r1.astype(BF16)
        lo = (r1 - mid.astype(F32)).astype(BF16)
        b_w = _dot(tri, jnp.concatenate([hi, mid, lo], axis=0))
        parts = []
        off = None
        for j in range(ng):
            bj = b_w[:, j * LANES:(j + 1) * LANES]
            if off is not None:
                bj = bj + off
            parts.append(bj)
            off = bj[g - 1:g]
        return lf, jnp.concatenate(parts, axis=0), off

    def intra_weights(i, hd, lf, b, b_end):
        rows, cols = where(i, hd)
        qb = q_ref[0, rows, cols]
        kb = k_ref[0, rows, cols]
        q = qb.astype(F32)
        k = kb.astype(F32)
        lev = lev_ref[...]
        halves = (slice(0, hc), slice(hc, c))
        a_diag = [jnp.zeros((hc, hc), F32), jnp.zeros((hc, hc), F32)]
        a10 = None
        row = lax.broadcasted_iota(jnp.int32, q.shape, 0)
        for l, (m, e) in enumerate(_hgrn_level_exponents(b, lf, c)):
            half = m // 2
            if half >= F32_SUBLANES:
                mixed = jnp.concatenate(
                    [piece for lo in range(0, c, m) for piece in (k[lo:lo + half], q[lo + half:lo + m])],
                    axis=0)
            else:
                mixed = jnp.where(row % m >= half, q, k)
            r = (mixed * jnp.exp2(e)).astype(BF16)
            if l == 0:
                a10 = _dot_nt(r[hc:], r[:hc])
                continue
            for j, rs in enumerate(halves):
                prod = _dot_nt(r[rs], r[rs])
                if half >= F32_SUBLANES:
                    pieces = []
                    for lo in range(0, hc, m):
                        low = slice(lo + half, lo + m)
                        pieces += [a_diag[j][lo:lo + half],
                                   jnp.where(lev[low] == l, prod[low], a_diag[j][low])]
                    a_diag[j] = jnp.concatenate(pieces, axis=0)
                else:
                    a_diag[j] = jnp.where(lev == l, prod, a_diag[j])
        here = lev == n_lev
        for j, rs in enumerate(halves):
            a_diag[j] = jnp.where(here, _dot_nt(qb[rs], kb[rs]), a_diag[j])
        a = jnp.concatenate(
            [jnp.concatenate([a_diag[0], jnp.zeros((hc, hc), F32)], axis=1),
             jnp.concatenate([a10, a_diag[1]], axis=1)], axis=0)
        q_in = (q * jnp.exp2(b)).astype(BF16)
        k_out = (k * jnp.exp2(b_end - b)).astype(BF16)
        return a.astype(BF16), q_in, k_out

    def read_out(i, hd, b_end, a, q_in, k_out, st):
        rows, cols = where(i, hd)
        v = v_ref[0, rows, cols]
        o = _dot_nt(q_in, st.astype(BF16)) + _dot(a, v)
        o = _rms(o, w_ref[:, cols]) * g_ref[0, rows, cols].astype(F32)
        o_ref[0, rows, cols] = o.astype(BF16)
        return st * jnp.exp2(b_end) + _dot_tn(v, k_out)

    init = tuple(jnp.zeros((HGRN_HEAD, HGRN_HEAD), F32) for _ in range(n_hd))

    def trip(t, states):
        units = [(BLOCKS_PER_TRIP * t + j, hd) for j in range(BLOCKS_PER_TRIP) for hd in range(n_hd)]
        gates = [cumulative_gates(i, hd) for i, hd in units]
        states = list(states)
        pending = None
        for u, (i, hd) in enumerate(units):
            lf, b, b_end = gates[u]
            ready = (i, hd, b_end) + intra_weights(i, hd, lf, b, b_end)
            if pending is not None:
                states[pending[1]] = read_out(*pending, states[pending[1]])
            pending = ready
        states[pending[1]] = read_out(*pending, states[pending[1]])
        return tuple(states)

    lax.fori_loop(0, n_blocks // BLOCKS_PER_TRIP, trip, init)


def _hgrn2(hq, kk, hi, lf, g, w_norm, *, heads_per_step):
    b, s, hw = hq.shape
    c = HGRN_BLOCK
    width = heads_per_step * HGRN_HEAD
    tri_np, lev_np, n_lev = _hgrn_tables(c, HGRN_CUMSUM_GROUP)
    tri = jnp.asarray(tri_np, BF16)
    lev = jnp.asarray(lev_np)
    kern = functools.partial(_hgrn_kernel, c=c, g=HGRN_CUMSUM_GROUP, n_blocks=s // c,
                             n_hd=heads_per_step, n_lev=n_lev)
    tok = pl.BlockSpec((1, s, width), lambda i, j: (i, 0, j))
    units = BLOCKS_PER_TRIP * heads_per_step
    vmem = _vmem_limit(
        pipelined=[((5, s, width), BF16), ((s, width), F32), (tri.shape, BF16), (lev.shape, jnp.int32)],
        temporaries=units * (2 * _nbytes((c, HGRN_HEAD), F32) + _nbytes((c, c), F32)
                             + 4 * _nbytes((c, HGRN_HEAD), BF16)))
    return pl.pallas_call(
        kern,
        grid=(b, hw // width),
        in_specs=[tok, tok, tok, tok, tok,
                  pl.BlockSpec((1, width), lambda i, j: (0, j)),
                  pl.BlockSpec(tri.shape, lambda i, j: (0, 0)),
                  pl.BlockSpec(lev.shape, lambda i, j: (0, 0))],
        out_specs=tok,
        out_shape=jax.ShapeDtypeStruct((b, s, hw), BF16),
        compiler_params=pltpu.CompilerParams(
            dimension_semantics=("parallel", "parallel"), vmem_limit_bytes=vmem),
        name="hgrn2",
    )(hq, kk, hi, lf, g, w_norm, tri, lev)


def _out_ffn_kernel(om_ref, oh_ref, x_ref, wo_ref, postw_ref, prew_ref, wg_ref, wu_ref, wd_ref,
                    fpostw_ref, out_ref, *, n_sub):
    sub = x_ref.shape[0] // n_sub
    tiles = [slice(i * sub, (i + 1) * sub) for i in range(n_sub)]
    hs = []
    for rows in tiles:
        mix = jnp.concatenate([om_ref[rows, :], oh_ref[rows, :]], axis=-1)
        hs.append(x_ref[rows, :] + _rms(_dot(mix, wo_ref[...]), postw_ref[...]))
    ffs = []
    for h in hs:
        z = _rms(h, prew_ref[...]).astype(BF16)
        gate = _dot(z, wg_ref[...])
        ffs.append((gate * _sigmoid(gate) * _dot(z, wu_ref[...])).astype(BF16))
    for rows, h, ff in zip(tiles, hs, ffs):
        out_ref[rows, :] = h + _rms(_dot(ff, wd_ref[...]), fpostw_ref[...])


def _out_ffn(om, oh, x, wo, postw, prew, wg, wu, wd, fpostw, *, tm, n_sub):
    n_tok, d = x.shape
    kern = functools.partial(_out_ffn_kernel, n_sub=n_sub)

    def full(a):
        return pl.BlockSpec(a.shape, lambda i: (0,) * a.ndim, pipeline_mode=pl.Buffered(1))

    tok = lambda w: pl.BlockSpec((tm, w), lambda i: (i, 0))
    sub = tm // n_sub
    vmem = _vmem_limit(
        pipelined=[((tm, om.shape[1] + oh.shape[1]), BF16), ((2, tm, d), F32)],
        resident=[(a.shape, a.dtype) for a in (wo, wg, wu, wd)],
        temporaries=2 * (_nbytes((sub, d + 2 * wd.shape[0]), F32) + _nbytes((sub, wd.shape[0]), BF16)))
    return pl.pallas_call(
        kern,
        grid=(n_tok // tm,),
        in_specs=[tok(om.shape[1]), tok(oh.shape[1]), tok(d), full(wo), full(postw), full(prew),
                  full(wg), full(wu), full(wd), full(fpostw)],
        out_specs=tok(d),
        out_shape=jax.ShapeDtypeStruct((n_tok, d), F32),
        compiler_params=pltpu.CompilerParams(
            dimension_semantics=("parallel",), vmem_limit_bytes=vmem),
        name="out_ffn",
    )(om, oh, x, wo, postw, prew, wg, wu, wd, fpostw)


def _prep_in_proj_weights(w_in, w_uq, w_ukv, *, q_rank, kv_rank):
    d = w_in.shape[0]
    half = MLA_ROPE // 2
    s2 = q_rank + kv_rank
    s3 = s2 + MLA_ROPE
    kr = w_in[:, s2:s3]
    z_lo = jnp.zeros((d, MLA_NOPE), w_in.dtype)
    kr_slab = jnp.concatenate([z_lo, kr, kr[:, half:], kr[:, :half]], axis=1)
    win_ext = jnp.concatenate([w_in[:, :s2], kr_slab, w_in[:, s3:]], axis=1).astype(BF16)

    n_heads = w_uq.shape[1]
    rope = w_uq[:, :, MLA_NOPE:]
    wq_pad = jnp.concatenate([w_uq, rope[:, :, half:], rope[:, :, :half]], axis=2)
    wq_pad = wq_pad.reshape(q_rank, n_heads * LANES).astype(BF16)

    wk_pad = jnp.pad(w_ukv[:, :, :MLA_NOPE], ((0, 0), (0, 0), (0, LANES - MLA_NOPE)))
    wv = w_ukv[:, :, MLA_NOPE:]
    wkv_cat = jnp.concatenate([wk_pad.reshape(kv_rank, n_heads * LANES),
                               wv.reshape(kv_rank, n_heads * MLA_V)], axis=1).astype(BF16)
    return win_ext, wq_pad, wkv_cat


def kernel(x, positions, attn_pre_norm, w_in, mla_q_norm, mla_w_uq, mla_kv_norm, mla_w_ukv, mla_out_norm,
           hgrn_lb_logits, hgrn_out_norm, w_out, attn_post_norm, ffn_pre_norm, w_gate, w_up, w_down,
           ffn_post_norm):
    b, s, d = x.shape
    assert attn_pre_norm.shape[0] == 1, "single-layer block"
    assert mla_w_uq.shape[3] == MLA_NOPE + MLA_ROPE and 2 * MLA_ROPE + MLA_NOPE == LANES
    q_rank = mla_q_norm.shape[-1]
    kv_rank = mla_kv_norm.shape[-1]
    n_heads = mla_w_uq.shape[2]
    hw = hgrn_out_norm.shape[-1]
    row = lambda a: a.reshape(1, -1)

    cos_t, sin_t = _rope_tables(positions)
    win_ext, wq_pad, wkv_cat = _prep_in_proj_weights(w_in[0], mla_w_uq[0], mla_w_ukv[0],
                                                     q_rank=q_rank, kv_rank=kv_rank)
    qt, k, vt, hq, lf, kk, hi, g = _in_proj(
        x, cos_t, sin_t, row(attn_pre_norm[0]), win_ext, row(mla_q_norm[0]), wq_pad,
        row(mla_kv_norm[0]), wkv_cat, hgrn_lb_logits, n_heads=n_heads, hw=hw, tm=1024, n_sub=4)
    o_mla = _mla_attention(qt, k, vt, row(mla_out_norm[0]), tq=512)
    o_hgrn = _hgrn2(hq, kk, hi, lf, g, row(hgrn_out_norm[0]), heads_per_step=2)

    out = _out_ffn(o_mla.reshape(b * s, -1), o_hgrn.reshape(b * s, -1), x.reshape(b * s, d),
                   w_out[0].astype(BF16), row(attn_post_norm[0]), row(ffn_pre_norm[0]),
                   w_gate[0].astype(BF16), w_up[0].astype(BF16), w_down[0].astype(BF16),
                   row(ffn_post_norm[0]), tm=1024, n_sub=4)
    return out.reshape(b, s, d)
```

```python
import functools
import math

import numpy as np
import jax
import jax.numpy as jnp
from jax import lax
from jax.experimental import pallas as pl
from jax.experimental.pallas import tpu as pltpu

F32 = jnp.float32
BF16 = jnp.bfloat16

EPS = 1e-6
ROPE_THETA = 10000.0
LOG2E = math.log2(math.e)
LANES = 128
BF16_SUBLANES = 16
MLA_NOPE = 64
MLA_ROPE = 32
MLA_V = 64
ONES_ROWS = BF16_SUBLANES
F32_SUBLANES = 8
HGRN_HEAD = 128
HGRN_BLOCK = 256
HGRN_CUMSUM_GROUP = 64
BLOCKS_PER_TRIP = 4
NEG_BIG = -1e30
MIB = 1024 * 1024
V7X_VMEM_BYTES = 64 * MIB
MIN_VMEM_REQUEST = 3 * V7X_VMEM_BYTES // 4


def _nbytes(shape, dtype):
    return math.prod(shape) * jnp.dtype(dtype).itemsize


def _vmem_limit(pipelined, resident=(), temporaries=0):
    total = 2 * sum(_nbytes(s, d) for s, d in pipelined) + sum(_nbytes(s, d) for s, d in resident)
    return min(max(total + temporaries, MIN_VMEM_REQUEST), V7X_VMEM_BYTES)


def _rms(x, w):
    inv = lax.rsqrt(jnp.mean(x * x, axis=-1, keepdims=True) + EPS)
    return (x * inv) * w


def _sigmoid(x):
    return 1.0 / (1.0 + jnp.exp(-x))


def _dot(a, b):
    return jnp.dot(a, b, preferred_element_type=F32)


def _dot_nt(a, b):
    return lax.dot_general(a, b, (((1,), (1,)), ((), ())), preferred_element_type=F32)


def _dot_tn(a, b):
    return lax.dot_general(a, b, (((0,), (0,)), ((), ())), preferred_element_type=F32)


def _rope_kernel(pos_ref, invf_ref, cos_ref, sin_ref):
    ang = pos_ref[...].astype(F32) * invf_ref[...]
    cos_ref[...] = jnp.cos(ang)
    sin_ref[...] = jnp.sin(ang)


def _rope_tables(positions):
    n_tok = positions.size
    half = MLA_ROPE // 2
    inv_freq = 1.0 / (ROPE_THETA ** (jnp.arange(0, MLA_ROPE, 2, dtype=F32) / MLA_ROPE))
    blk = min(n_tok, 4096)
    spec = pl.BlockSpec((half, blk), lambda i: (0, i))
    shp = jax.ShapeDtypeStruct((half, n_tok), F32)
    return pl.pallas_call(
        _rope_kernel,
        grid=(n_tok // blk,),
        in_specs=[pl.BlockSpec((1, blk), lambda i: (0, i)), pl.BlockSpec((half, 1), lambda i: (0, 0))],
        out_specs=[spec, spec],
        out_shape=[shp, shp],
        name="rope_tables",
    )(positions.reshape(1, n_tok), inv_freq.reshape(half, 1))


def _inproj_kernel(x_ref, ct_ref, st_ref, prew_ref, win_ref, qnw_ref, wq_ref,
                   kvnw_ref, wkv_ref, lbl_ref,
                   qt_out, k_out, vt_out, hq_out, lf_out, kk_out, hi_out, g_out,
                   *, qscale, n_heads, q_rank, kv_rank, hw, n_sub):
    sub = x_ref.shape[1] // n_sub
    tiles = [slice(i * sub, (i + 1) * sub) for i in range(n_sub)]
    r0, r1 = MLA_NOPE, MLA_NOPE + MLA_ROPE
    hp = n_heads * LANES
    c0 = q_rank + kv_rank + LANES

    lbl = lbl_ref[...]
    e = jnp.exp(lbl - jnp.max(lbl, axis=0, keepdims=True))
    lb = e[0:1] / jnp.sum(e, axis=0, keepdims=True)

    us = [_rms(x_ref[0, rows, :], prew_ref[...]).astype(BF16) for rows in tiles]
    lows = [_dot(u, win_ref[:, :c0]) for u in us]

    def gate_group(g, u, rows):
        y = _dot(u, win_ref[:, c0 + g * hw:c0 + (g + 1) * hw])
        if g == 0:
            hq_out[0, rows, :] = (y * _sigmoid(y)).astype(BF16)
        elif g == 1:
            f = lb + (1.0 - lb) * _sigmoid(y)
            lf_out[0, rows, :] = jnp.log(f) * LOG2E
            kk_out[0, rows, :] = (1.0 - f).astype(BF16)
        elif g == 2:
            hi_out[0, rows, :] = y.astype(BF16)
        else:
            g_out[0, rows, :] = (y * _sigmoid(y)).astype(BF16)

    def mla_group(low, rows):
        c_q = low[:, :q_rank]
        c_kv = low[:, q_rank:q_rank + kv_rank]
        kr = low[:, q_rank + kv_rank:]
        ct = ct_ref[:, rows]
        st = st_ref[:, rows]
        cc = jnp.concatenate([ct, ct], axis=0)
        ss = jnp.concatenate([-st, st], axis=0)

        cqn = _rms(c_q, qnw_ref[...]).astype(BF16)
        qq = _dot(cqn, wq_ref[...])
        ccq = cc * qscale
        ssq = ss * qscale
        zpad = jnp.zeros((LANES - r1, sub), F32)
        for h in range(n_heads):
            xt = qq[:, h * LANES:(h + 1) * LANES].T
            rope = xt[r0:r1] * ccq + xt[r1:] * ssq
            qt_out[0, h, :, rows] = jnp.concatenate([xt[:r0] * qscale, rope, zpad], axis=0).astype(BF16)

        z64 = jnp.zeros((r0, sub), F32)
        ck = jnp.concatenate([z64, cc, zpad], axis=0).T
        sk = jnp.concatenate([z64, zpad, ss], axis=0).T
        k_rope = kr * ck + pltpu.roll(kr * sk, LANES - MLA_ROPE, 1)

        ckvn = _rms(c_kv, kvnw_ref[...]).astype(BF16)
        kvv = _dot(ckvn, wkv_ref[...])
        for h in range(n_heads):
            k_out[0, h, rows, :] = (kvv[:, h * LANES:(h + 1) * LANES] + k_rope).astype(BF16)
        vt_out[0, :, rows] = kvv[:, hp:].T.astype(BF16)

    for u, low, rows in zip(us, lows, tiles):
        gate_group(0, u, rows)
        mla_group(low, rows)
    for g in (1, 2, 3):
        for u, rows in zip(us, tiles):
            gate_group(g, u, rows)


def _in_proj(x, cos_t, sin_t, prew, win_ext, qnw, wq_pad, kvnw, wkv_cat, lbl, *, n_heads, hw, tm, n_sub):
    b, s, d = x.shape
    q_rank = qnw.shape[-1]
    kv_rank = kvnw.shape[-1]
    ns = s // tm
    qscale = (MLA_NOPE + MLA_ROPE) ** -0.5 * LOG2E
    kern = functools.partial(_inproj_kernel, qscale=qscale, n_heads=n_heads, q_rank=q_rank,
                             kv_rank=kv_rank, hw=hw, n_sub=n_sub)

    def full(a):
        return pl.BlockSpec(a.shape, lambda i, j: (0,) * a.ndim)

    tok = lambda w: pl.BlockSpec((1, tm, w), lambda i, j: (i, j, 0))
    head = pl.BlockSpec((1, n_heads, tm, LANES), lambda i, j: (i, 0, j, 0))
    head_t = pl.BlockSpec((1, n_heads, LANES, tm), lambda i, j: (i, 0, 0, j))
    tab_t = pl.BlockSpec((cos_t.shape[0], tm), lambda i, j: (0, i * ns + j))
    vw = n_heads * MLA_V
    out_shape = [
        jax.ShapeDtypeStruct((b, n_heads, LANES, s), BF16),
        jax.ShapeDtypeStruct((b, n_heads, s, LANES), BF16),
        jax.ShapeDtypeStruct((b, vw, s), BF16),
        jax.ShapeDtypeStruct((b, s, hw), BF16),
        jax.ShapeDtypeStruct((b, s, hw), F32),
        jax.ShapeDtypeStruct((b, s, hw), BF16),
        jax.ShapeDtypeStruct((b, s, hw), BF16),
        jax.ShapeDtypeStruct((b, s, hw), BF16),
    ]
    sub = tm // n_sub
    vmem = _vmem_limit(
        pipelined=[((tm, d), F32), ((2 * cos_t.shape[0], tm), F32), ((2, n_heads, tm, LANES), BF16),
                   ((vw, tm), BF16), ((tm, hw), F32), ((4, tm, hw), BF16)]
        + [(a.shape, a.dtype) for a in (win_ext, wq_pad, wkv_cat)],
        temporaries=2 * _nbytes((sub, win_ext.shape[1] - 3 * hw + wq_pad.shape[1] + wkv_cat.shape[1] + hw), F32))
    return pl.pallas_call(
        kern,
        grid=(b, ns),
        in_specs=[tok(d), tab_t, tab_t, full(prew), full(win_ext), full(qnw), full(wq_pad),
                  full(kvnw), full(wkv_cat), full(lbl)],
        out_specs=[head_t, head, pl.BlockSpec((1, vw, tm), lambda i, j: (i, 0, j)),
                   tok(hw), tok(hw), tok(hw), tok(hw), tok(hw)],
        out_shape=out_shape,
        compiler_params=pltpu.CompilerParams(
            dimension_semantics=("parallel", "parallel"), vmem_limit_bytes=vmem),
        name="in_proj",
    )(x, cos_t, sin_t, prew, win_ext, qnw, wq_pad, kvnw, wkv_cat, lbl)


def _attn_kernel(qt_ref, k_ref, vt_ref, w_ref, o_ref, sa_ref, sb_ref, *, tq, n_q):
    kc = tq // 2
    ones = jnp.ones((ONES_ROWS, kc), BF16)
    nu = range(2)

    def scores(u, ks, s_ref, q_start):
        s_t = _dot(k_ref[0, u, pl.ds(ks, kc), :], qt_ref[0, u, :, pl.ds(q_start, tq)])
        s_ref[u] = s_t
        return jnp.max(s_t, axis=0, keepdims=True)

    def scores_right_half(u, ks, s_ref, q_start):
        s_ref[u, :, kc:] = _dot(k_ref[0, u, pl.ds(ks, kc), :], qt_ref[0, u, :, pl.ds(q_start + kc, kc)])

    def q_tile(qi, cm_first):
        q0 = pl.multiple_of(qi * tq, tq)

        def values(u, ks):
            return jnp.concatenate([vt_ref[0, u * MLA_V:(u + 1) * MLA_V, pl.ds(ks, kc)], ones], axis=0)

        def causal(s_sq):
            key = lax.broadcasted_iota(jnp.int32, s_sq.shape, 0)
            qry = lax.broadcasted_iota(jnp.int32, s_sq.shape, 1)
            return jnp.where(key <= qry, s_sq, -jnp.inf)

        def update(s_t, cmax, m, acc, v_ext):
            m_new = jnp.maximum(m, cmax)
            p_t = jnp.exp2(s_t - m_new).astype(BF16)
            return m_new, jnp.exp2(m - m_new) * acc + _dot(v_ext, p_t)

        def accumulate(u, ks, s_ref, cmax, m, acc):
            return update(s_ref[u], cmax, m, acc, values(u, ks))

        def accumulate_first_diagonal(u, s_ref, cmax, m, acc):
            s_t = s_ref[u]
            left = causal(s_t[:, :kc])
            s_t = jnp.concatenate([left, s_t[:, kc:]], axis=1)
            cmax = jnp.concatenate([jnp.max(left, axis=0, keepdims=True), cmax[:, kc:]], axis=1)
            return update(s_t, cmax, m, acc, values(u, q0))

        def accumulate_second_diagonal(u, s_ref, m, acc):
            s_t = causal(s_ref[u, :, kc:])
            m_r, acc_r = update(s_t, jnp.max(s_t, axis=0, keepdims=True), m[:, kc:], acc[:, kc:],
                                values(u, q0 + kc))
            return (jnp.concatenate([m[:, :kc], m_r], axis=1),
                    jnp.concatenate([acc[:, :kc], acc_r], axis=1))

        def pair_body(t, carry):
            k0 = pl.multiple_of(t * tq, tq)
            cm_b = [scores(u, k0 + kc, sb_ref, q0) for u in nu]
            st = [accumulate(u, k0, sa_ref, carry[u][0], carry[u][1], carry[u][2]) for u in nu]
            cm_a = [scores(u, k0 + tq, sa_ref, q0) for u in nu]
            st = [accumulate(u, k0 + kc, sb_ref, cm_b[u], st[u][0], st[u][1]) for u in nu]
            return tuple((cm_a[u], st[u][0], st[u][1]) for u in nu)

        init = tuple((cm_first[u], jnp.full((1, tq), NEG_BIG, F32),
                      jnp.zeros((MLA_V + ONES_ROWS, tq), F32)) for u in nu)
        carry = lax.fori_loop(0, qi // 2, lambda t, c: pair_body(2 * t + 1, pair_body(2 * t, c)), init)
        carry = lax.fori_loop(qi - qi % 2, qi, pair_body, carry)

        for u in nu:
            scores_right_half(u, q0 + kc, sb_ref, q0)
        st = [accumulate_first_diagonal(u, sa_ref, carry[u][0], carry[u][1], carry[u][2]) for u in nu]
        q_next = pl.multiple_of(jnp.minimum(qi + 1, n_q - 1) * tq, tq)
        cm_next = [scores(u, 0, sa_ref, q_next) for u in nu[:-1]]
        st = [accumulate_second_diagonal(u, sb_ref, st[u][0], st[u][1]) for u in nu]
        cm_next = tuple(cm_next + [scores(nu[-1], 0, sa_ref, q_next)])
        outs = []
        for u in nu:
            acc = st[u][1]
            o = acc[:MLA_V] / acc[MLA_V:MLA_V + 1]
            outs.append(o * lax.rsqrt(jnp.mean(o * o, axis=0, keepdims=True) + EPS))
        pair = jnp.concatenate(outs, axis=0).T
        o_ref[0, pl.ds(q0, tq), :] = (pair * w_ref[...]).astype(BF16)
        return cm_next

    lax.fori_loop(0, n_q, q_tile, tuple(scores(u, 0, sa_ref, 0) for u in nu))


def _mla_attention(qt, k, vt, w_norm, *, tq):
    b, n_heads, _, s = qt.shape
    kern = functools.partial(_attn_kernel, tq=tq, n_q=s // tq)
    s_scratch = pltpu.VMEM((2, tq // 2, tq), F32)
    vmem = _vmem_limit(
        pipelined=[((2, LANES, s), BF16), ((2, s, LANES), BF16), ((2 * MLA_V, s), BF16), ((s, LANES), BF16)],
        resident=[(s_scratch.shape, F32)] * 2,
        temporaries=4 * (_nbytes((MLA_V + ONES_ROWS, tq), F32) + _nbytes((tq // 2, tq), F32)
                         + _nbytes((tq // 2, tq), BF16)))
    return pl.pallas_call(
        kern,
        grid=(b, n_heads // 2),
        in_specs=[
            pl.BlockSpec((1, 2, LANES, s), lambda i, j: (i, j, 0, 0)),
            pl.BlockSpec((1, 2, s, LANES), lambda i, j: (i, j, 0, 0)),
            pl.BlockSpec((1, 2 * MLA_V, s), lambda i, j: (i, j, 0)),
            pl.BlockSpec((1, LANES), lambda i, j: (0, j)),
        ],
        out_specs=pl.BlockSpec((1, s, LANES), lambda i, j: (i, 0, j)),
        out_shape=jax.ShapeDtypeStruct((b, s, n_heads * MLA_V), BF16),
        scratch_shapes=[s_scratch, s_scratch],
        compiler_params=pltpu.CompilerParams(
            dimension_semantics=("parallel", "parallel"), vmem_limit_bytes=vmem),
        name="mla_attn",
    )(qt, k, vt, w_norm)


def _hgrn_tables(c, g):
    n_lev = int(np.log2(c))
    t = np.arange(c)[:, None]
    s = np.arange(c)[None, :]
    lev = np.full((c, c), n_lev + 1, np.int32)
    for l in range(n_lev):
        m = c >> l
        half = m // 2
        lev[(t // m == s // m) & (t % m >= half) & (s % m < half)] = l
    lev[t == s] = n_lev
    hc = c // 2
    assert (lev[:hc, :hc] == lev[hc:, hc:]).all() and (lev[hc:, :hc] == 0).all()
    tri = np.tril(np.ones((g, g), np.float32))
    return np.concatenate([tri, tri, tri], axis=1), lev[:hc, :hc], n_lev


def _hgrn_level_exponents(b, lf, c):
    sub = lax.broadcasted_iota(jnp.int32, (F32_SUBLANES, LANES), 0)
    out = []
    m = c
    while m >= 2:
        half = m // 2
        if half >= F32_SUBLANES:
            pieces = []
            for lo in range(0, c, m):
                mid = lo + half
                r = b[mid - 1:mid]
                pieces += [r - b[lo:mid], b[mid:mid + half] - r]
            e = jnp.concatenate(pieces, axis=0)
        elif m == 2:
            odd = lax.broadcasted_iota(jnp.int32, lf.shape, 0) % 2 == 1
            e = jnp.where(odd, lf, 0.0)
        else:
            sign = jnp.where(sub % m >= half, 1.0, -1.0)
            pieces = []
            for v0 in range(0, c, F32_SUBLANES):
                ref = b[v0 + half - 1:v0 + half]
                for j in range(1, F32_SUBLANES // m):
                    ref = jnp.where(sub >= j * m, b[v0 + j * m + half - 1:v0 + j * m + half], ref)
                pieces.append(sign * (b[v0:v0 + F32_SUBLANES] - ref))
            e = jnp.concatenate(pieces, axis=0)
        out.append((m, e))
        m = half
    return out


def _hgrn_kernel(q_ref, k_ref, v_ref, lf_ref, g_ref, w_ref, tri_ref, lev_ref, o_ref,
                 *, c, g, n_blocks, n_hd, n_lev):
    tri = tri_ref[...]
    ng = c // g
    hc = c // 2

    def where(i, hd):
        return pl.ds(pl.multiple_of(i * c, c), c), slice(hd * HGRN_HEAD, (hd + 1) * HGRN_HEAD)

    def cumulative_gates(i, hd):
        rows, cols = where(i, hd)
        lf = lf_ref[0, rows, cols]
        lf_w = jnp.concatenate([lf[j * g:(j + 1) * g] for j in range(ng)], axis=1)
        hi = lf_w.astype(BF16)
        r1 = lf_w - hi.astype(F32)
        mid = r1.astype(BF16)
        lo = (r1 - mid.astype(F32)).astype(BF16)
        b_w = _dot(tri, jnp.concatenate([hi, mid, lo], axis=0))
        parts = []
        off = None
        for j in range(ng):
            bj = b_w[:, j * LANES:(j + 1) * LANES]
            if off is not None:
                bj = bj + off
            parts.append(bj)
            off = bj[g - 1:g]
        return lf, jnp.concatenate(parts, axis=0), off

    def intra_weights(i, hd, lf, b, b_end):
        rows, cols = where(i, hd)
        qb = q_ref[0, rows, cols]
        kb = k_ref[0, rows, cols]
        q = qb.astype(F32)
        k = kb.astype(F32)
        lev = lev_ref[...]
        halves = (slice(0, hc), slice(hc, c))
        a_diag = [jnp.zeros((hc, hc), F32), jnp.zeros((hc, hc), F32)]
        a10 = None
        row = lax.broadcasted_iota(jnp.int32, q.shape, 0)
        for l, (m, e) in enumerate(_hgrn_level_exponents(b, lf, c)):
            half = m // 2
            if half >= F32_SUBLANES:
                mixed = jnp.concatenate(
                    [piece for lo in range(0, c, m) for piece in (k[lo:lo + half], q[lo + half:lo + m])],
                    axis=0)
            else:
                mixed = jnp.where(row % m >= half, q, k)
            r = (mixed * jnp.exp2(e)).astype(BF16)
            if l == 0:
                a10 = _dot_nt(r[hc:], r[:hc])
                continue
            for j, rs in enumerate(halves):
                prod = _dot_nt(r[rs], r[rs])
                if half >= F32_SUBLANES:
                    pieces = []
                    for lo in range(0, hc, m):
                        low = slice(lo + half, lo + m)
                        pieces += [a_diag[j][lo:lo + half],
                                   jnp.where(lev[low] == l, prod[low], a_diag[j][low])]
                    a_diag[j] = jnp.concatenate(pieces, axis=0)
                else:
                    a_diag[j] = jnp.where(lev == l, prod, a_diag[j])
        here = lev == n_lev
        for j, rs in enumerate(halves):
            a_diag[j] = jnp.where(here, _dot_nt(qb[rs], kb[rs]), a_diag[j])
        a = jnp.concatenate(
            [jnp.concatenate([a_diag[0], jnp.zeros((hc, hc), F32)], axis=1),
             jnp.concatenate([a10, a_diag[1]], axis=1)], axis=0)
        q_in = (q * jnp.exp2(b)).astype(BF16)
        k_out = (k * jnp.exp2(b_end - b)).astype(BF16)
        return a.astype(BF16), q_in, k_out

    def read_out(i, hd, b_end, a, q_in, k_out, st):
        rows, cols = where(i, hd)
        v = v_ref[0, rows, cols]
        o = _dot_nt(q_in, st.astype(BF16)) + _dot(a, v)
        o = _rms(o, w_ref[:, cols]) * g_ref[0, rows, cols].astype(F32)
        o_ref[0, rows, cols] = o.astype(BF16)
        return st * jnp.exp2(b_end) + _dot_tn(v, k_out)

    init = tuple(jnp.zeros((HGRN_HEAD, HGRN_HEAD), F32) for _ in range(n_hd))

    def trip(t, states):
        units = [(BLOCKS_PER_TRIP * t + j, hd) for j in range(BLOCKS_PER_TRIP) for hd in range(n_hd)]
        gates = [cumulative_gates(i, hd) for i, hd in units]
        states = list(states)
        pending = None
        for u, (i, hd) in enumerate(units):
            lf, b, b_end = gates[u]
            ready = (i, hd, b_end) + intra_weights(i, hd, lf, b, b_end)
            if pending is not None:
                states[pending[1]] = read_out(*pending, states[pending[1]])
            pending = ready
        states[pending[1]] = read_out(*pending, states[pending[1]])
        return tuple(states)

    lax.fori_loop(0, n_blocks // BLOCKS_PER_TRIP, trip, init)


def _hgrn2(hq, kk, hi, lf, g, w_norm, *, heads_per_step):
    b, s, hw = hq.shape
    c = HGRN_BLOCK
    width = heads_per_step * HGRN_HEAD
    tri_np, lev_np, n_lev = _hgrn_tables(c, HGRN_CUMSUM_GROUP)
    tri = jnp.asarray(tri_np, BF16)
    lev = jnp.asarray(lev_np)
    kern = functools.partial(_hgrn_kernel, c=c, g=HGRN_CUMSUM_GROUP, n_blocks=s // c,
                             n_hd=heads_per_step, n_lev=n_lev)
    tok = pl.BlockSpec((1, s, width), lambda i, j: (i, 0, j))
    units = BLOCKS_PER_TRIP * heads_per_step
    vmem = _vmem_limit(
        pipelined=[((5, s, width), BF16), ((s, width), F32), (tri.shape, BF16), (lev.shape, jnp.int32)],
        temporaries=units * (2 * _nbytes((c, HGRN_HEAD), F32) + _nbytes((c, c), F32)
                             + 4 * _nbytes((c, HGRN_HEAD), BF16)))
    return pl.pallas_call(
        kern,
        grid=(b, hw // width),
        in_specs=[tok, tok, tok, tok, tok,
                  pl.BlockSpec((1, width), lambda i, j: (0, j)),
                  pl.BlockSpec(tri.shape, lambda i, j: (0, 0)),
                  pl.BlockSpec(lev.shape, lambda i, j: (0, 0))],
        out_specs=tok,
        out_shape=jax.ShapeDtypeStruct((b, s, hw), BF16),
        compiler_params=pltpu.CompilerParams(
            dimension_semantics=("parallel", "parallel"), vmem_limit_bytes=vmem),
        name="hgrn2",
    )(hq, kk, hi, lf, g, w_norm, tri, lev)


def _out_ffn_kernel(om_ref, oh_ref, x_ref, wo_ref, postw_ref, prew_ref, wg_ref, wu_ref, wd_ref,
                    fpostw_ref, out_ref, *, n_sub):
    sub = x_ref.shape[0] // n_sub
    tiles = [slice(i * sub, (i + 1) * sub) for i in range(n_sub)]
    hs = []
    for rows in tiles:
        mix = jnp.concatenate([om_ref[rows, :], oh_ref[rows, :]], axis=-1)
        hs.append(x_ref[rows, :] + _rms(_dot(mix, wo_ref[...]), postw_ref[...]))
    ffs = []
    for h in hs:
        z = _rms(h, prew_ref[...]).astype(BF16)
        gate = _dot(z, wg_ref[...])
        ffs.append((gate * _sigmoid(gate) * _dot(z, wu_ref[...])).astype(BF16))
    for rows, h, ff in zip(tiles, hs, ffs):
        out_ref[rows, :] = h + _rms(_dot(ff, wd_ref[...]), fpostw_ref[...])


def _out_ffn(om, oh, x, wo, postw, prew, wg, wu, wd, fpostw, *, tm, n_sub):
    n_tok, d = x.shape
    kern = functools.partial(_out_ffn_kernel, n_sub=n_sub)

    def full(a):
        return pl.BlockSpec(a.shape, lambda i: (0,) * a.ndim, pipeline_mode=pl.Buffered(1))

    tok = lambda w: pl.BlockSpec((tm, w), lambda i: (i, 0))
    sub = tm // n_sub
    vmem = _vmem_limit(
        pipelined=[((tm, om.shape[1] + oh.shape[1]), BF16), ((2, tm, d), F32)],
        resident=[(a.shape, a.dtype) for a in (wo, wg, wu, wd)],
        temporaries=2 * (_nbytes((sub, d + 2 * wd.shape[0]), F32) + _nbytes((sub, wd.shape[0]), BF16)))
    return pl.pallas_call(
        kern,
        grid=(n_tok // tm,),
        in_specs=[tok(om.shape[1]), tok(oh.shape[1]), tok(d), full(wo), full(postw), full(prew),
                  full(wg), full(wu), full(wd), full(fpostw)],
        out_specs=tok(d),
        out_shape=jax.ShapeDtypeStruct((n_tok, d), F32),
        compiler_params=pltpu.CompilerParams(
            dimension_semantics=("parallel",), vmem_limit_bytes=vmem),
        name="out_ffn",
    )(om, oh, x, wo, postw, prew, wg, wu, wd, fpostw)


def _prep_in_proj_weights(w_in, w_uq, w_ukv, *, q_rank, kv_rank):
    d = w_in.shape[0]
    half = MLA_ROPE // 2
    s2 = q_rank + kv_rank
    s3 = s2 + MLA_ROPE
    kr = w_in[:, s2:s3]
    z_lo = jnp.zeros((d, MLA_NOPE), w_in.dtype)
    kr_slab = jnp.concatenate([z_lo, kr, kr[:, half:], kr[:, :half]], axis=1)
    win_ext = jnp.concatenate([w_in[:, :s2], kr_slab, w_in[:, s3:]], axis=1).astype(BF16)

    n_heads = w_uq.shape[1]
    rope = w_uq[:, :, MLA_NOPE:]
    wq_pad = jnp.concatenate([w_uq, rope[:, :, half:], rope[:, :, :half]], axis=2)
    wq_pad = wq_pad.reshape(q_rank, n_heads * LANES).astype(BF16)

    wk_pad = jnp.pad(w_ukv[:, :, :MLA_NOPE], ((0, 0), (0, 0), (0, LANES - MLA_NOPE)))
    wv = w_ukv[:, :, MLA_NOPE:]
    wkv_cat = jnp.concatenate([wk_pad.reshape(kv_rank, n_heads * LANES),
                               wv.reshape(kv_rank, n_heads * MLA_V)], axis=1).astype(BF16)
    return win_ext, wq_pad, wkv_cat


def kernel(x, positions, attn_pre_norm, w_in, mla_q_norm, mla_w_uq, mla_kv_norm, mla_w_ukv, mla_out_norm,
           hgrn_lb_logits, hgrn_out_norm, w_out, attn_post_norm, ffn_pre_norm, w_gate, w_up, w_down,
           ffn_post_norm):
    b, s, d = x.shape
    assert attn_pre_norm.shape[0] == 1, "single-layer block"
    assert mla_w_uq.shape[3] == MLA_NOPE + MLA_ROPE and 2 * MLA_ROPE + MLA_NOPE == LANES
    q_rank = mla_q_norm.shape[-1]
    kv_rank = mla_kv_norm.shape[-1]
    n_heads = mla_w_uq.shape[2]
    hw = hgrn_out_norm.shape[-1]
    row = lambda a: a.reshape(1, -1)

    cos_t, sin_t = _rope_tables(positions)
    win_ext, wq_pad, wkv_cat = _prep_in_proj_weights(w_in[0], mla_w_uq[0], mla_w_ukv[0],
                                                     q_rank=q_rank, kv_rank=kv_rank)
    qt, k, vt, hq, lf, kk, hi, g = _in_proj(
        x, cos_t, sin_t, row(attn_pre_norm[0]), win_ext, row(mla_q_norm[0]), wq_pad,
        row(mla_kv_norm[0]), wkv_cat, hgrn_lb_logits, n_heads=n_heads, hw=hw, tm=1024, n_sub=4)
    o_mla = _mla_attention(qt, k, vt, row(mla_out_norm[0]), tq=512)
    o_hgrn = _hgrn2(hq, kk, hi, lf, g, row(hgrn_out_norm[0]), heads_per_step=2)

    out = _out_ffn(o_mla.reshape(b * s, -1), o_hgrn.reshape(b * s, -1), x.reshape(b * s, d),
                   w_out[0].astype(BF16), row(attn_post_norm[0]), row(ffn_pre_norm[0]),
                   w_gate[0].astype(BF16), w_up[0].astype(BF16), w_down[0].astype(BF16),
                   row(ffn_post_norm[0]), tm=1024, n_sub=4)
    return out.reshape(b, s, d)
```

```python
import functools
import math

import numpy as np
import jax
import jax.numpy as jnp
from jax import lax
from jax.experimental import pallas as pl
from jax.experimental.pallas import tpu as pltpu

F32 = jnp.float32
BF16 = jnp.bfloat16

EPS = 1e-6
ROPE_THETA = 10000.0
LOG2E = math.log2(math.e)
LANES = 128
BF16_SUBLANES = 16
MLA_NOPE = 64
MLA_ROPE = 32
MLA_V = 64
ONES_ROWS = BF16_SUBLANES
F32_SUBLANES = 8
HGRN_HEAD = 128
HGRN_BLOCK = 256
HGRN_CUMSUM_GROUP = 64
BLOCKS_PER_TRIP = 4
NEG_BIG = -1e30
MIB = 1024 * 1024
V7X_VMEM_BYTES = 64 * MIB
MIN_VMEM_REQUEST = 3 * V7X_VMEM_BYTES // 4


def _nbytes(shape, dtype):
    return math.prod(shape) * jnp.dtype(dtype).itemsize


def _vmem_limit(pipelined, resident=(), temporaries=0):
    total = 2 * sum(_nbytes(s, d) for s, d in pipelined) + sum(_nbytes(s, d) for s, d in resident)
    return min(max(total + temporaries, MIN_VMEM_REQUEST), V7X_VMEM_BYTES)


def _rms(x, w):
    inv = lax.rsqrt(jnp.mean(x * x, axis=-1, keepdims=True) + EPS)
    return (x * inv) * w


def _sigmoid(x):
    return 1.0 / (1.0 + jnp.exp(-x))


def _dot(a, b):
    return jnp.dot(a, b, preferred_element_type=F32)


def _dot_nt(a, b):
    return lax.dot_general(a, b, (((1,), (1,)), ((), ())), preferred_element_type=F32)


def _dot_tn(a, b):
    return lax.dot_general(a, b, (((0,), (0,)), ((), ())), preferred_element_type=F32)


def _rope_kernel(pos_ref, invf_ref, cos_ref, sin_ref):
    ang = pos_ref[...].astype(F32) * invf_ref[...]
    cos_ref[...] = jnp.cos(ang)
    sin_ref[...] = jnp.sin(ang)


def _rope_tables(positions):
    n_tok = positions.size
    half = MLA_ROPE // 2
    inv_freq = 1.0 / (ROPE_THETA ** (jnp.arange(0, MLA_ROPE, 2, dtype=F32) / MLA_ROPE))
    blk = min(n_tok, 4096)
    spec = pl.BlockSpec((half, blk), lambda i: (0, i))
    shp = jax.ShapeDtypeStruct((half, n_tok), F32)
    return pl.pallas_call(
        _rope_kernel,
        grid=(n_tok // blk,),
        in_specs=[pl.BlockSpec((1, blk), lambda i: (0, i)), pl.BlockSpec((half, 1), lambda i: (0, 0))],
        out_specs=[spec, spec],
        out_shape=[shp, shp],
        name="rope_tables",
    )(positions.reshape(1, n_tok), inv_freq.reshape(half, 1))


def _inproj_kernel(x_ref, ct_ref, st_ref, prew_ref, win_ref, qnw_ref, wq_ref,
                   kvnw_ref, wkv_ref, lbl_ref,
                   qt_out, k_out, vt_out, hq_out, lf_out, kk_out, hi_out, g_out,
                   *, qscale, n_heads, q_rank, kv_rank, hw, n_sub):
    sub = x_ref.shape[1] // n_sub
    tiles = [slice(i * sub, (i + 1) * sub) for i in range(n_sub)]
    r0, r1 = MLA_NOPE, MLA_NOPE + MLA_ROPE
    hp = n_heads * LANES
    c0 = q_rank + kv_rank + LANES

    lbl = lbl_ref[...]
    e = jnp.exp(lbl - jnp.max(lbl, axis=0, keepdims=True))
    lb = e[0:1] / jnp.sum(e, axis=0, keepdims=True)

    us = [_rms(x_ref[0, rows, :], prew_ref[...]).astype(BF16) for rows in tiles]
    lows = [_dot(u, win_ref[:, :c0]) for u in us]

    def gate_group(g, u, rows):
        y = _dot(u, win_ref[:, c0 + g * hw:c0 + (g + 1) * hw])
        if g == 0:
            hq_out[0, rows, :] = (y * _sigmoid(y)).astype(BF16)
        elif g == 1:
            f = lb + (1.0 - lb) * _sigmoid(y)
            lf_out[0, rows, :] = jnp.log(f) * LOG2E
            kk_out[0, rows, :] = (1.0 - f).astype(BF16)
        elif g == 2:
            hi_out[0, rows, :] = y.astype(BF16)
        else:
            g_out[0, rows, :] = (y * _sigmoid(y)).astype(BF16)

    def mla_group(low, rows):
        c_q = low[:, :q_rank]
        c_kv = low[:, q_rank:q_rank + kv_rank]
        kr = low[:, q_rank + kv_rank:]
        ct = ct_ref[:, rows]
        st = st_ref[:, rows]
        cc = jnp.concatenate([ct, ct], axis=0)
        ss = jnp.concatenate([-st, st], axis=0)

        cqn = _rms(c_q, qnw_ref[...]).astype(BF16)
        qq = _dot(cqn, wq_ref[...])
        ccq = cc * qscale
        ssq = ss * qscale
        zpad = jnp.zeros((LANES - r1, sub), F32)
        for h in range(n_heads):
            xt = qq[:, h * LANES:(h + 1) * LANES].T
            rope = xt[r0:r1] * ccq + xt[r1:] * ssq
            qt_out[0, h, :, rows] = jnp.concatenate([xt[:r0] * qscale, rope, zpad], axis=0).astype(BF16)

        z64 = jnp.zeros((r0, sub), F32)
        ck = jnp.concatenate([z64, cc, zpad], axis=0).T
        sk = jnp.concatenate([z64, zpad, ss], axis=0).T
        k_rope = kr * ck + pltpu.roll(kr * sk, LANES - MLA_ROPE, 1)

        ckvn = _rms(c_kv, kvnw_ref[...]).astype(BF16)
        kvv = _dot(ckvn, wkv_ref[...])
        for h in range(n_heads):
            k_out[0, h, rows, :] = (kvv[:, h * LANES:(h + 1) * LANES] + k_rope).astype(BF16)
        vt_out[0, :, rows] = kvv[:, hp:].T.astype(BF16)

    for u, low, rows in zip(us, lows, tiles):
        gate_group(0, u, rows)
        mla_group(low, rows)
    for g in (1, 2, 3):
        for u, rows in zip(us, tiles):
            gate_group(g, u, rows)


def _in_proj(x, cos_t, sin_t, prew, win_ext, qnw, wq_pad, kvnw, wkv_cat, lbl, *, n_heads, hw, tm, n_sub):
    b, s, d = x.shape
    q_rank = qnw.shape[-1]
    kv_rank = kvnw.shape[-1]
    ns = s // tm
    qscale = (MLA_NOPE + MLA_ROPE) ** -0.5 * LOG2E
    kern = functools.partial(_inproj_kernel, qscale=qscale, n_heads=n_heads, q_rank=q_rank,
                             kv_rank=kv_rank, hw=hw, n_sub=n_sub)

    def full(a):
        return pl.BlockSpec(a.shape, lambda i, j: (0,) * a.ndim)

    tok = lambda w: pl.BlockSpec((1, tm, w), lambda i, j: (i, j, 0))
    head = pl.BlockSpec((1, n_heads, tm, LANES), lambda i, j: (i, 0, j, 0))
    head_t = pl.BlockSpec((1, n_heads, LANES, tm), lambda i, j: (i, 0, 0, j))
    tab_t = pl.BlockSpec((cos_t.shape[0], tm), lambda i, j: (0, i * ns + j))
    vw = n_heads * MLA_V
    out_shape = [
        jax.ShapeDtypeStruct((b, n_heads, LANES, s), BF16),
        jax.ShapeDtypeStruct((b, n_heads, s, LANES), BF16),
        jax.ShapeDtypeStruct((b, vw, s), BF16),
        jax.ShapeDtypeStruct((b, s, hw), BF16),
        jax.ShapeDtypeStruct((b, s, hw), F32),
        jax.ShapeDtypeStruct((b, s, hw), BF16),
        jax.ShapeDtypeStruct((b, s, hw), BF16),
        jax.ShapeDtypeStruct((b, s, hw), BF16),
    ]
    sub = tm // n_sub
    vmem = _vmem_limit(
        pipelined=[((tm, d), F32), ((2 * cos_t.shape[0], tm), F32), ((2, n_heads, tm, LANES), BF16),
                   ((vw, tm), BF16), ((tm, hw), F32), ((4, tm, hw), BF16)]
        + [(a.shape, a.dtype) for a in (win_ext, wq_pad, wkv_cat)],
        temporaries=2 * _nbytes((sub, win_ext.shape[1] - 3 * hw + wq_pad.shape[1] + wkv_cat.shape[1] + hw), F32))
    return pl.pallas_call(
        kern,
        grid=(b, ns),
        in_specs=[tok(d), tab_t, tab_t, full(prew), full(win_ext), full(qnw), full(wq_pad),
                  full(kvnw), full(wkv_cat), full(lbl)],
        out_specs=[head_t, head, pl.BlockSpec((1, vw, tm), lambda i, j: (i, 0, j)),
                   tok(hw), tok(hw), tok(hw), tok(hw), tok(hw)],
        out_shape=out_shape,
        compiler_params=pltpu.CompilerParams(
            dimension_semantics=("parallel", "parallel"), vmem_limit_bytes=vmem),
        name="in_proj",
    )(x, cos_t, sin_t, prew, win_ext, qnw, wq_pad, kvnw, wkv_cat, lbl)


def _attn_kernel(qt_ref, k_ref, vt_ref, w_ref, o_ref, sa_ref, sb_ref, *, tq, n_q):
    kc = tq // 2
    ones = jnp.ones((ONES_ROWS, kc), BF16)
    nu = range(2)

    def scores(u, ks, s_ref, q_start):
        s_t = _dot(k_ref[0, u, pl.ds(ks, kc), :], qt_ref[0, u, :, pl.ds(q_start, tq)])
        s_ref[u] = s_t
        return jnp.max(s_t, axis=0, keepdims=True)

    def scores_right_half(u, ks, s_ref, q_start):
        s_ref[u, :, kc:] = _dot(k_ref[0, u, pl.ds(ks, kc), :], qt_ref[0, u, :, pl.ds(q_start + kc, kc)])

    def values(u, ks):
        return jnp.concatenate([vt_ref[0, u * MLA_V:(u + 1) * MLA_V, pl.ds(ks, kc)], ones], axis=0)

    def causal(s_sq):
        key = lax.broadcasted_iota(jnp.int32, s_sq.shape, 0)
        qry = lax.broadcasted_iota(jnp.int32, s_sq.shape, 1)
        return jnp.where(key <= qry, s_sq, -jnp.inf)

    def update(s_t, cmax, m, acc, v_ext):
        m_new = jnp.maximum(m, cmax)
        p_t = jnp.exp2(s_t - m_new).astype(BF16)
        return m_new, jnp.exp2(m - m_new) * acc + _dot(v_ext, p_t)

    def accumulate(u, ks, s_ref, cmax, m, acc):
        return update(s_ref[u], cmax, m, acc, values(u, ks))

    def accumulate_first_diagonal(u, q0, s_ref, cmax, m, acc):
        s_t = s_ref[u]
        left = causal(s_t[:, :kc])
        s_t = jnp.concatenate([left, s_t[:, kc:]], axis=1)
        cmax = jnp.concatenate([jnp.max(left, axis=0, keepdims=True), cmax[:, kc:]], axis=1)
        return update(s_t, cmax, m, acc, values(u, q0))

    def accumulate_second_diagonal(u, q0, s_ref, m, acc):
        s_t = causal(s_ref[u, :, kc:])
        m_r, acc_r = update(s_t, jnp.max(s_t, axis=0, keepdims=True), m[:, kc:], acc[:, kc:],
                            values(u, q0 + kc))
        return (jnp.concatenate([m[:, :kc], m_r], axis=1),
                jnp.concatenate([acc[:, :kc], acc_r], axis=1))

    def pair_body(q0, t, carry):
        k0 = t * tq if isinstance(t, int) else pl.multiple_of(t * tq, tq)
        cm_b = [scores(u, k0 + kc, sb_ref, q0) for u in nu]
        st = [accumulate(u, k0, sa_ref, carry[u][0], carry[u][1], carry[u][2]) for u in nu]
        cm_a = [scores(u, k0 + tq, sa_ref, q0) for u in nu]
        st = [accumulate(u, k0 + kc, sb_ref, cm_b[u], st[u][0], st[u][1]) for u in nu]
        return tuple((cm_a[u], st[u][0], st[u][1]) for u in nu)

    def fresh(cm_first):
        return tuple((cm_first[u], jnp.full((1, tq), NEG_BIG, F32),
                      jnp.zeros((MLA_V + ONES_ROWS, tq), F32)) for u in nu)

    def finish(q0, carry, q_next):
        for u in nu:
            scores_right_half(u, q0 + kc, sb_ref, q0)
        st = [accumulate_first_diagonal(u, q0, sa_ref, carry[u][0], carry[u][1], carry[u][2]) for u in nu]
        cm_next = [] if q_next is None else [scores(u, 0, sa_ref, q_next) for u in nu[:-1]]
        st = [accumulate_second_diagonal(u, q0, sb_ref, st[u][0], st[u][1]) for u in nu]
        if q_next is not None:
            cm_next.append(scores(nu[-1], 0, sa_ref, q_next))
        return tuple(cm_next), [st[u][1] for u in nu]

    def normalise_store(q0, accs):
        outs = []
        for acc in accs:
            o = acc[:MLA_V] / acc[MLA_V:MLA_V + 1]
            outs.append(o * lax.rsqrt(jnp.mean(o * o, axis=0, keepdims=True) + EPS))
        pair = jnp.concatenate(outs, axis=0).T
        o_ref[0, pl.ds(q0, tq), :] = (pair * w_ref[...]).astype(BF16)

    def tile(t_idx, carry):
        q_prev = pl.multiple_of((t_idx - 1) * tq, tq)
        q0 = pl.multiple_of(t_idx * tq, tq)
        cm0, accs = finish(q_prev, carry, q0)
        carry = pair_body(q0, 0, fresh(cm0))
        normalise_store(q_prev, accs)
        rest = t_idx - 1
        carry = lax.fori_loop(
            0, rest // 2, lambda t, c: pair_body(q0, 2 * t + 2, pair_body(q0, 2 * t + 1, c)), carry)
        return lax.fori_loop(t_idx - rest % 2, t_idx, functools.partial(pair_body, q0), carry)

    carry = fresh(tuple(scores(u, 0, sa_ref, 0) for u in nu))
    carry = lax.fori_loop(1, n_q, tile, carry)
    normalise_store((n_q - 1) * tq, finish((n_q - 1) * tq, carry, None)[1])


def _mla_attention(qt, k, vt, w_norm, *, tq):
    b, n_heads, _, s = qt.shape
    kern = functools.partial(_attn_kernel, tq=tq, n_q=s // tq)
    s_scratch = pltpu.VMEM((2, tq // 2, tq), F32)
    vmem = _vmem_limit(
        pipelined=[((2, LANES, s), BF16), ((2, s, LANES), BF16), ((2 * MLA_V, s), BF16), ((s, LANES), BF16)],
        resident=[(s_scratch.shape, F32)] * 2,
        temporaries=4 * (_nbytes((MLA_V + ONES_ROWS, tq), F32) + _nbytes((tq // 2, tq), F32)
                         + _nbytes((tq // 2, tq), BF16)))
    return pl.pallas_call(
        kern,
        grid=(b, n_heads // 2),
        in_specs=[
            pl.BlockSpec((1, 2, LANES, s), lambda i, j: (i, j, 0, 0)),
            pl.BlockSpec((1, 2, s, LANES), lambda i, j: (i, j, 0, 0)),
            pl.BlockSpec((1, 2 * MLA_V, s), lambda i, j: (i, j, 0)),
            pl.BlockSpec((1, LANES), lambda i, j: (0, j)),
        ],
        out_specs=pl.BlockSpec((1, s, LANES), lambda i, j: (i, 0, j)),
        out_shape=jax.ShapeDtypeStruct((b, s, n_heads * MLA_V), BF16),
        scratch_shapes=[s_scratch, s_scratch],
        compiler_params=pltpu.CompilerParams(
            dimension_semantics=("parallel", "parallel"), vmem_limit_bytes=vmem),
        name="mla_attn",
    )(qt, k, vt, w_norm)


def _hgrn_tables(c, g):
    n_lev = int(np.log2(c))
    t = np.arange(c)[:, None]
    s = np.arange(c)[None, :]
    lev = np.full((c, c), n_lev + 1, np.int32)
    for l in range(n_lev):
        m = c >> l
        half = m // 2
        lev[(t // m == s // m) & (t % m >= half) & (s % m < half)] = l
    lev[t == s] = n_lev
    hc = c // 2
    assert (lev[:hc, :hc] == lev[hc:, hc:]).all() and (lev[hc:, :hc] == 0).all()
    tri = np.tril(np.ones((g, g), np.float32))
    return np.concatenate([tri, tri, tri], axis=1), lev[:hc, :hc], n_lev


def _hgrn_level_exponents(b, lf, c):
    sub = lax.broadcasted_iota(jnp.int32, (F32_SUBLANES, LANES), 0)
    out = []
    m = c
    while m >= 2:
        half = m // 2
        if half >= F32_SUBLANES:
            pieces = []
            for lo in range(0, c, m):
                mid = lo + half
                r = b[mid - 1:mid]
                pieces += [r - b[lo:mid], b[mid:mid + half] - r]
            e = jnp.concatenate(pieces, axis=0)
        elif m == 2:
            odd = lax.broadcasted_iota(jnp.int32, lf.shape, 0) % 2 == 1
            e = jnp.where(odd, lf, 0.0)
        else:
            sign = jnp.where(sub % m >= half, 1.0, -1.0)
            pieces = []
            for v0 in range(0, c, F32_SUBLANES):
                ref = b[v0 + half - 1:v0 + half]
                for j in range(1, F32_SUBLANES // m):
                    ref = jnp.where(sub >= j * m, b[v0 + j * m + half - 1:v0 + j * m + half], ref)
                pieces.append(sign * (b[v0:v0 + F32_SUBLANES] - ref))
            e = jnp.concatenate(pieces, axis=0)
        out.append((m, e))
        m = half
    return out


def _hgrn_kernel(q_ref, k_ref, v_ref, lf_ref, g_ref, w_ref, tri_ref, lev_ref, o_ref,
                 *, c, g, n_blocks, n_hd, n_lev):
    tri = tri_ref[...]
    ng = c // g
    hc = c // 2

    def where(i, hd):
        return pl.ds(pl.multiple_of(i * c, c), c), slice(hd * HGRN_HEAD, (hd + 1) * HGRN_HEAD)

    def cumulative_gates(i, hd):
        rows, cols = where(i, hd)
        lf = lf_ref[0, rows, cols]
        lf_w = jnp.concatenate([lf[j * g:(j + 1) * g] for j in range(ng)], axis=1)
        hi = lf_w.astype(BF16)
        r1 = lf_w - hi.astype(F32)
        mid = r1.astype(BF16)
        lo = (r1 - mid.astype(F32)).astype(BF16)
        b_w = _dot(tri, jnp.concatenate([hi, mid, lo], axis=0))
        parts = []
        off = None
        for j in range(ng):
            bj = b_w[:, j * LANES:(j + 1) * LANES]
            if off is not None:
                bj = bj + off
            parts.append(bj)
            off = bj[g - 1:g]
        return lf, jnp.concatenate(parts, axis=0), off

    def intra_weights(i, hd, lf, b, b_end):
        rows, cols = where(i, hd)
        qb = q_ref[0, rows, cols]
        kb = k_ref[0, rows, cols]
        q = qb.astype(F32)
        k = kb.astype(F32)
        lev = lev_ref[...]
        halves = (slice(0, hc), slice(hc, c))
        a_diag = [jnp.zeros((hc, hc), F32), jnp.zeros((hc, hc), F32)]
        a10 = None
        row = lax.broadcasted_iota(jnp.int32, q.shape, 0)
        for l, (m, e) in enumerate(_hgrn_level_exponents(b, lf, c)):
            half = m // 2
            if half >= F32_SUBLANES:
                mixed = jnp.concatenate(
                    [piece for lo in range(0, c, m) for piece in (k[lo:lo + half], q[lo + half:lo + m])],
                    axis=0)
            else:
                mixed = jnp.where(row % m >= half, q, k)
            r = (mixed * jnp.exp2(e)).astype(BF16)
            if l == 0:
                a10 = _dot_nt(r[hc:], r[:hc])
                continue
            for j, rs in enumerate(halves):
                prod = _dot_nt(r[rs], r[rs])
                if half >= F32_SUBLANES:
                    pieces = []
                    for lo in range(0, hc, m):
                        low = slice(lo + half, lo + m)
                        pieces += [a_diag[j][lo:lo + half],
                                   jnp.where(lev[low] == l, prod[low], a_diag[j][low])]
                    a_diag[j] = jnp.concatenate(pieces, axis=0)
                else:
                    a_diag[j] = jnp.where(lev == l, prod, a_diag[j])
        here = lev == n_lev
        for j, rs in enumerate(halves):
            a_diag[j] = jnp.where(here, _dot_nt(qb[rs], kb[rs]), a_diag[j])
        a = jnp.concatenate(
            [jnp.concatenate([a_diag[0], jnp.zeros((hc, hc), F32)], axis=1),
             jnp.concatenate([a10, a_diag[1]], axis=1)], axis=0)
        q_in = (q * jnp.exp2(b)).astype(BF16)
        k_out = (k * jnp.exp2(b_end - b)).astype(BF16)
        return a.astype(BF16), q_in, k_out

    def read_out(i, hd, b_end, a, q_in, k_out, st):
        rows, cols = where(i, hd)
        v = v_ref[0, rows, cols]
        o = _dot_nt(q_in, st.astype(BF16)) + _dot(a, v)
        o = _rms(o, w_ref[:, cols]) * g_ref[0, rows, cols].astype(F32)
        o_ref[0, rows, cols] = o.astype(BF16)
        return st * jnp.exp2(b_end) + _dot_tn(v, k_out)

    init = tuple(jnp.zeros((HGRN_HEAD, HGRN_HEAD), F32) for _ in range(n_hd))

    def trip(t, states):
        units = [(BLOCKS_PER_TRIP * t + j, hd) for j in range(BLOCKS_PER_TRIP) for hd in range(n_hd)]
        gates = [cumulative_gates(i, hd) for i, hd in units]
        states = list(states)
        pending = None
        for u, (i, hd) in enumerate(units):
            lf, b, b_end = gates[u]
            ready = (i, hd, b_end) + intra_weights(i, hd, lf, b, b_end)
            if pending is not None:
                states[pending[1]] = read_out(*pending, states[pending[1]])
            pending = ready
        states[pending[1]] = read_out(*pending, states[pending[1]])
        return tuple(states)

    lax.fori_loop(0, n_blocks // BLOCKS_PER_TRIP, trip, init)


def _hgrn2(hq, kk, hi, lf, g, w_norm, *, heads_per_step):
    b, s, hw = hq.shape
    c = HGRN_BLOCK
    width = heads_per_step * HGRN_HEAD
    tri_np, lev_np, n_lev = _hgrn_tables(c, HGRN_CUMSUM_GROUP)
    tri = jnp.asarray(tri_np, BF16)
    lev = jnp.asarray(lev_np)
    kern = functools.partial(_hgrn_kernel, c=c, g=HGRN_CUMSUM_GROUP, n_blocks=s // c,
                             n_hd=heads_per_step, n_lev=n_lev)
    tok = pl.BlockSpec((1, s, width), lambda i, j: (i, 0, j))
    units = BLOCKS_PER_TRIP * heads_per_step
    vmem = _vmem_limit(
        pipelined=[((5, s, width), BF16), ((s, width), F32), (tri.shape, BF16), (lev.shape, jnp.int32)],
        temporaries=units * (2 * _nbytes((c, HGRN_HEAD), F32) + _nbytes((c, c), F32)
                             + 4 * _nbytes((c, HGRN_HEAD), BF16)))
    return pl.pallas_call(
        kern,
        grid=(b, hw // width),
        in_specs=[tok, tok, tok, tok, tok,
                  pl.BlockSpec((1, width), lambda i, j: (0, j)),
                  pl.BlockSpec(tri.shape, lambda i, j: (0, 0)),
                  pl.BlockSpec(lev.shape, lambda i, j: (0, 0))],
        out_specs=tok,
        out_shape=jax.ShapeDtypeStruct((b, s, hw), BF16),
        compiler_params=pltpu.CompilerParams(
            dimension_semantics=("parallel", "parallel"), vmem_limit_bytes=vmem),
        name="hgrn2",
    )(hq, kk, hi, lf, g, w_norm, tri, lev)


def _out_ffn_kernel(om_ref, oh_ref, x_ref, wo_ref, postw_ref, prew_ref, wg_ref, wu_ref, wd_ref,
                    fpostw_ref, out_ref, *, n_sub):
    sub = x_ref.shape[0] // n_sub
    tiles = [slice(i * sub, (i + 1) * sub) for i in range(n_sub)]
    hs = []
    for rows in tiles:
        mix = jnp.concatenate([om_ref[rows, :], oh_ref[rows, :]], axis=-1)
        hs.append(x_ref[rows, :] + _rms(_dot(mix, wo_ref[...]), postw_ref[...]))
    ffs = []
    for h in hs:
        z = _rms(h, prew_ref[...]).astype(BF16)
        gate = _dot(z, wg_ref[...])
        ffs.append((gate * _sigmoid(gate) * _dot(z, wu_ref[...])).astype(BF16))
    for rows, h, ff in zip(tiles, hs, ffs):
        out_ref[rows, :] = h + _rms(_dot(ff, wd_ref[...]), fpostw_ref[...])


def _out_ffn(om, oh, x, wo, postw, prew, wg, wu, wd, fpostw, *, tm, n_sub):
    n_tok, d = x.shape
    kern = functools.partial(_out_ffn_kernel, n_sub=n_sub)

    def full(a):
        return pl.BlockSpec(a.shape, lambda i: (0,) * a.ndim, pipeline_mode=pl.Buffered(1))

    tok = lambda w: pl.BlockSpec((tm, w), lambda i: (i, 0))
    sub = tm // n_sub
    vmem = _vmem_limit(
        pipelined=[((tm, om.shape[1] + oh.shape[1]), BF16), ((2, tm, d), F32)],
        resident=[(a.shape, a.dtype) for a in (wo, wg, wu, wd)],
        temporaries=2 * (_nbytes((sub, d + 2 * wd.shape[0]), F32) + _nbytes((sub, wd.shape[0]), BF16)))
    return pl.pallas_call(
        kern,
        grid=(n_tok // tm,),
        in_specs=[tok(om.shape[1]), tok(oh.shape[1]), tok(d), full(wo), full(postw), full(prew),
                  full(wg), full(wu), full(wd), full(fpostw)],
        out_specs=tok(d),
        out_shape=jax.ShapeDtypeStruct((n_tok, d), F32),
        compiler_params=pltpu.CompilerParams(
            dimension_semantics=("parallel",), vmem_limit_bytes=vmem),
        name="out_ffn",
    )(om, oh, x, wo, postw, prew, wg, wu, wd, fpostw)


def _prep_in_proj_weights(w_in, w_uq, w_ukv, *, q_rank, kv_rank):
    d = w_in.shape[0]
    half = MLA_ROPE // 2
    s2 = q_rank + kv_rank
    s3 = s2 + MLA_ROPE
    kr = w_in[:, s2:s3]
    z_lo = jnp.zeros((d, MLA_NOPE), w_in.dtype)
    kr_slab = jnp.concatenate([z_lo, kr, kr[:, half:], kr[:, :half]], axis=1)
    win_ext = jnp.concatenate([w_in[:, :s2], kr_slab, w_in[:, s3:]], axis=1).astype(BF16)

    n_heads = w_uq.shape[1]
    rope = w_uq[:, :, MLA_NOPE:]
    wq_pad = jnp.concatenate([w_uq, rope[:, :, half:], rope[:, :, :half]], axis=2)
    wq_pad = wq_pad.reshape(q_rank, n_heads * LANES).astype(BF16)

    wk_pad = jnp.pad(w_ukv[:, :, :MLA_NOPE], ((0, 0), (0, 0), (0, LANES - MLA_NOPE)))
    wv = w_ukv[:, :, MLA_NOPE:]
    wkv_cat = jnp.concatenate([wk_pad.reshape(kv_rank, n_heads * LANES),
                               wv.reshape(kv_rank, n_heads * MLA_V)], axis=1).astype(BF16)
    return win_ext, wq_pad, wkv_cat


def kernel(x, positions, attn_pre_norm, w_in, mla_q_norm, mla_w_uq, mla_kv_norm, mla_w_ukv, mla_out_norm,
           hgrn_lb_logits, hgrn_out_norm, w_out, attn_post_norm, ffn_pre_norm, w_gate, w_up, w_down,
           ffn_post_norm):
    b, s, d = x.shape
    assert attn_pre_norm.shape[0] == 1, "single-layer block"
    assert mla_w_uq.shape[3] == MLA_NOPE + MLA_ROPE and 2 * MLA_ROPE + MLA_NOPE == LANES
    q_rank = mla_q_norm.shape[-1]
    kv_rank = mla_kv_norm.shape[-1]
    n_heads = mla_w_uq.shape[2]
    hw = hgrn_out_norm.shape[-1]
    row = lambda a: a.reshape(1, -1)

    cos_t, sin_t = _rope_tables(positions)
    win_ext, wq_pad, wkv_cat = _prep_in_proj_weights(w_in[0], mla_w_uq[0], mla_w_ukv[0],
                                                     q_rank=q_rank, kv_rank=kv_rank)
    qt, k, vt, hq, lf, kk, hi, g = _in_proj(
        x, cos_t, sin_t, row(attn_pre_norm[0]), win_ext, row(mla_q_norm[0]), wq_pad,
        row(mla_kv_norm[0]), wkv_cat, hgrn_lb_logits, n_heads=n_heads, hw=hw, tm=1024, n_sub=4)
    o_mla = _mla_attention(qt, k, vt, row(mla_out_norm[0]), tq=512)
    o_hgrn = _hgrn2(hq, kk, hi, lf, g, row(hgrn_out_norm[0]), heads_per_step=2)

    out = _out_ffn(o_mla.reshape(b * s, -1), o_hgrn.reshape(b * s, -1), x.reshape(b * s, d),
                   w_out[0].astype(BF16), row(attn_post_norm[0]), row(ffn_pre_norm[0]),
                   w_gate[0].astype(BF16), w_up[0].astype(BF16), w_down[0].astype(BF16),
                   row(ffn_post_norm[0]), tm=1024, n_sub=4)
    return out.reshape(b, s, d)
```

```python
import functools
import math

import numpy as np
import jax
import jax.numpy as jnp
from jax import lax
from jax.experimental import pallas as pl
from jax.experimental.pallas import tpu as pltpu

F32 = jnp.float32
BF16 = jnp.bfloat16

EPS = 1e-6
ROPE_THETA = 10000.0
LOG2E = math.log2(math.e)
LANES = 128
BF16_SUBLANES = 16
MLA_NOPE = 64
MLA_ROPE = 32
MLA_V = 64
ONES_ROWS = BF16_SUBLANES
F32_SUBLANES = 8
HGRN_HEAD = 128
HGRN_BLOCK = 256
HGRN_CUMSUM_GROUP = 64
BLOCKS_PER_TRIP = 4
NEG_BIG = -1e30
MIB = 1024 * 1024
V7X_VMEM_BYTES = 64 * MIB
MIN_VMEM_REQUEST = 3 * V7X_VMEM_BYTES // 4


def _nbytes(shape, dtype):
    return math.prod(shape) * jnp.dtype(dtype).itemsize


def _vmem_limit(pipelined, resident=(), temporaries=0):
    total = 2 * sum(_nbytes(s, d) for s, d in pipelined) + sum(_nbytes(s, d) for s, d in resident)
    return min(max(total + temporaries, MIN_VMEM_REQUEST), V7X_VMEM_BYTES)


def _rms(x, w):
    inv = lax.rsqrt(jnp.mean(x * x, axis=-1, keepdims=True) + EPS)
    return (x * inv) * w


def _sigmoid(x):
    return 1.0 / (1.0 + jnp.exp(-x))


def _dot(a, b):
    return jnp.dot(a, b, preferred_element_type=F32)


def _dot_nt(a, b):
    return lax.dot_general(a, b, (((1,), (1,)), ((), ())), preferred_element_type=F32)


def _dot_tn(a, b):
    return lax.dot_general(a, b, (((0,), (0,)), ((), ())), preferred_element_type=F32)


def _inproj_kernel(x_ref, pos_ref, invf_ref, prew_ref, win_ref, qnw_ref, wq_ref,
                   kvnw_ref, wkv_ref, lbl_ref,
                   qt_out, k_out, vt_out, hq_out, lf_out, kk_out, hi_out, g_out,
                   *, qscale, n_heads, q_rank, kv_rank, hw, n_sub):
    sub = x_ref.shape[1] // n_sub
    tiles = [slice(i * sub, (i + 1) * sub) for i in range(n_sub)]
    r0, r1 = MLA_NOPE, MLA_NOPE + MLA_ROPE
    hp = n_heads * LANES
    c0 = q_rank + kv_rank + LANES

    lbl = lbl_ref[...]
    e = jnp.exp(lbl - jnp.max(lbl, axis=0, keepdims=True))
    lb = e[0:1] / jnp.sum(e, axis=0, keepdims=True)

    us = [_rms(x_ref[0, rows, :], prew_ref[...]).astype(BF16) for rows in tiles]
    lows = [_dot(u, win_ref[:, :c0]) for u in us]

    def gate_group(g, u, rows):
        y = _dot(u, win_ref[:, c0 + g * hw:c0 + (g + 1) * hw])
        if g == 0:
            hq_out[0, rows, :] = (y * _sigmoid(y)).astype(BF16)
        elif g == 1:
            f = lb + (1.0 - lb) * _sigmoid(y)
            lf_out[0, rows, :] = jnp.log(f) * LOG2E
            kk_out[0, rows, :] = (1.0 - f).astype(BF16)
        elif g == 2:
            hi_out[0, rows, :] = y.astype(BF16)
        else:
            g_out[0, rows, :] = (y * _sigmoid(y)).astype(BF16)

    def mla_group(low, rows):
        c_q = low[:, :q_rank]
        c_kv = low[:, q_rank:q_rank + kv_rank]
        kr = low[:, q_rank + kv_rank:]
        ang = pos_ref[0, :, rows].astype(F32) * invf_ref[...]
        ct = jnp.cos(ang)
        st = jnp.sin(ang)
        cc = jnp.concatenate([ct, ct], axis=0)
        ss = jnp.concatenate([-st, st], axis=0)

        cqn = _rms(c_q, qnw_ref[...]).astype(BF16)
        qq = _dot(cqn, wq_ref[...])
        ccq = cc * qscale
        ssq = ss * qscale
        zpad = jnp.zeros((LANES - r1, sub), F32)
        for h in range(n_heads):
            xt = qq[:, h * LANES:(h + 1) * LANES].T
            rope = xt[r0:r1] * ccq + xt[r1:] * ssq
            qt_out[0, h, :, rows] = jnp.concatenate([xt[:r0] * qscale, rope, zpad], axis=0).astype(BF16)

        z64 = jnp.zeros((r0, sub), F32)
        ck = jnp.concatenate([z64, cc, zpad], axis=0).T
        sk = jnp.concatenate([z64, zpad, ss], axis=0).T
        k_rope = kr * ck + pltpu.roll(kr * sk, LANES - MLA_ROPE, 1)

        ckvn = _rms(c_kv, kvnw_ref[...]).astype(BF16)
        kvv = _dot(ckvn, wkv_ref[...])
        for h in range(n_heads):
            k_out[0, h, rows, :] = (kvv[:, h * LANES:(h + 1) * LANES] + k_rope).astype(BF16)
        vt_out[0, :, rows] = kvv[:, hp:].T.astype(BF16)

    for u, low, rows in zip(us, lows, tiles):
        gate_group(0, u, rows)
        mla_group(low, rows)
    for g in (1, 2, 3):
        for u, rows in zip(us, tiles):
            gate_group(g, u, rows)


def _in_proj(x, positions, prew, win_ext, qnw, wq_pad, kvnw, wkv_cat, lbl, *, n_heads, hw, tm, n_sub):
    b, s, d = x.shape
    q_rank = qnw.shape[-1]
    kv_rank = kvnw.shape[-1]
    ns = s // tm
    qscale = (MLA_NOPE + MLA_ROPE) ** -0.5 * LOG2E
    kern = functools.partial(_inproj_kernel, qscale=qscale, n_heads=n_heads, q_rank=q_rank,
                             kv_rank=kv_rank, hw=hw, n_sub=n_sub)

    def full(a):
        return pl.BlockSpec(a.shape, lambda i, j: (0,) * a.ndim)

    tok = lambda w: pl.BlockSpec((1, tm, w), lambda i, j: (i, j, 0))
    head = pl.BlockSpec((1, n_heads, tm, LANES), lambda i, j: (i, 0, j, 0))
    head_t = pl.BlockSpec((1, n_heads, LANES, tm), lambda i, j: (i, 0, 0, j))
    half = MLA_ROPE // 2
    inv_freq = (1.0 / (ROPE_THETA ** (jnp.arange(0, MLA_ROPE, 2, dtype=F32) / MLA_ROPE))).reshape(half, 1)
    pos_spec = pl.BlockSpec((1, 1, tm), lambda i, j: (i, 0, j))
    vw = n_heads * MLA_V
    out_shape = [
        jax.ShapeDtypeStruct((b, n_heads, LANES, s), BF16),
        jax.ShapeDtypeStruct((b, n_heads, s, LANES), BF16),
        jax.ShapeDtypeStruct((b, vw, s), BF16),
        jax.ShapeDtypeStruct((b, s, hw), BF16),
        jax.ShapeDtypeStruct((b, s, hw), F32),
        jax.ShapeDtypeStruct((b, s, hw), BF16),
        jax.ShapeDtypeStruct((b, s, hw), BF16),
        jax.ShapeDtypeStruct((b, s, hw), BF16),
    ]
    sub = tm // n_sub
    vmem = _vmem_limit(
        pipelined=[((tm, d), F32), ((F32_SUBLANES, tm), jnp.int32), ((2, n_heads, tm, LANES), BF16),
                   ((vw, tm), BF16), ((tm, hw), F32), ((4, tm, hw), BF16)]
        + [(a.shape, a.dtype) for a in (win_ext, wq_pad, wkv_cat)],
        temporaries=2 * _nbytes((sub, win_ext.shape[1] - 3 * hw + wq_pad.shape[1] + wkv_cat.shape[1] + hw), F32))
    return pl.pallas_call(
        kern,
        grid=(b, ns),
        in_specs=[tok(d), pos_spec, full(inv_freq), full(prew), full(win_ext), full(qnw), full(wq_pad),
                  full(kvnw), full(wkv_cat), full(lbl)],
        out_specs=[head_t, head, pl.BlockSpec((1, vw, tm), lambda i, j: (i, 0, j)),
                   tok(hw), tok(hw), tok(hw), tok(hw), tok(hw)],
        out_shape=out_shape,
        compiler_params=pltpu.CompilerParams(
            dimension_semantics=("parallel", "parallel"), vmem_limit_bytes=vmem),
        name="in_proj",
    )(x, positions.reshape(b, 1, s), inv_freq, prew, win_ext, qnw, wq_pad, kvnw, wkv_cat, lbl)


def _attn_kernel(qt_ref, k_ref, vt_ref, w_ref, o_ref, sa_ref, sb_ref, *, tq, n_q):
    kc = tq // 2
    ones = jnp.ones((ONES_ROWS, kc), BF16)
    nu = range(2)

    def scores(u, ks, s_ref, q_start):
        s_t = _dot(k_ref[0, u, pl.ds(ks, kc), :], qt_ref[0, u, :, pl.ds(q_start, tq)])
        s_ref[u] = s_t
        return jnp.max(s_t, axis=0, keepdims=True)

    def scores_right_half(u, ks, s_ref, q_start):
        s_ref[u, :, kc:] = _dot(k_ref[0, u, pl.ds(ks, kc), :], qt_ref[0, u, :, pl.ds(q_start + kc, kc)])

    def values(u, ks):
        return jnp.concatenate([vt_ref[0, u * MLA_V:(u + 1) * MLA_V, pl.ds(ks, kc)], ones], axis=0)

    def causal(s_sq):
        key = lax.broadcasted_iota(jnp.int32, s_sq.shape, 0)
        qry = lax.broadcasted_iota(jnp.int32, s_sq.shape, 1)
        return jnp.where(key <= qry, s_sq, -jnp.inf)

    def update(s_t, cmax, m, acc, v_ext):
        m_new = jnp.maximum(m, cmax)
        p_t = jnp.exp2(s_t - m_new).astype(BF16)
        return m_new, jnp.exp2(m - m_new) * acc + _dot(v_ext, p_t)

    def accumulate(u, ks, s_ref, cmax, m, acc):
        return update(s_ref[u], cmax, m, acc, values(u, ks))

    def accumulate_first_diagonal(u, q0, s_ref, cmax, m, acc):
        s_t = s_ref[u]
        left = causal(s_t[:, :kc])
        s_t = jnp.concatenate([left, s_t[:, kc:]], axis=1)
        cmax = jnp.concatenate([jnp.max(left, axis=0, keepdims=True), cmax[:, kc:]], axis=1)
        return update(s_t, cmax, m, acc, values(u, q0))

    def accumulate_second_diagonal(u, q0, s_ref, m, acc):
        s_t = causal(s_ref[u, :, kc:])
        m_r, acc_r = update(s_t, jnp.max(s_t, axis=0, keepdims=True), m[:, kc:], acc[:, kc:],
                            values(u, q0 + kc))
        return (jnp.concatenate([m[:, :kc], m_r], axis=1),
                jnp.concatenate([acc[:, :kc], acc_r], axis=1))

    def pair_body(q0, t, carry):
        k0 = t * tq if isinstance(t, int) else pl.multiple_of(t * tq, tq)
        cm_b = [scores(u, k0 + kc, sb_ref, q0) for u in nu]
        st = [accumulate(u, k0, sa_ref, carry[u][0], carry[u][1], carry[u][2]) for u in nu]
        cm_a = [scores(u, k0 + tq, sa_ref, q0) for u in nu]
        st = [accumulate(u, k0 + kc, sb_ref, cm_b[u], st[u][0], st[u][1]) for u in nu]
        return tuple((cm_a[u], st[u][0], st[u][1]) for u in nu)

    def fresh(cm_first):
        return tuple((cm_first[u], jnp.full((1, tq), NEG_BIG, F32),
                      jnp.zeros((MLA_V + ONES_ROWS, tq), F32)) for u in nu)

    def finish(q0, carry, q_next):
        for u in nu:
            scores_right_half(u, q0 + kc, sb_ref, q0)
        st = [accumulate_first_diagonal(u, q0, sa_ref, carry[u][0], carry[u][1], carry[u][2]) for u in nu]
        cm_next = [] if q_next is None else [scores(u, 0, sa_ref, q_next) for u in nu[:-1]]
        st = [accumulate_second_diagonal(u, q0, sb_ref, st[u][0], st[u][1]) for u in nu]
        if q_next is not None:
            cm_next.append(scores(nu[-1], 0, sa_ref, q_next))
        return tuple(cm_next), [st[u][1] for u in nu]

    def normalise_store(q0, accs):
        outs = []
        for acc in accs:
            o = acc[:MLA_V] / acc[MLA_V:MLA_V + 1]
            outs.append(o * lax.rsqrt(jnp.mean(o * o, axis=0, keepdims=True) + EPS))
        pair = jnp.concatenate(outs, axis=0).T
        o_ref[0, pl.ds(q0, tq), :] = (pair * w_ref[...]).astype(BF16)

    def tile(t_idx, carry):
        q_prev = pl.multiple_of((t_idx - 1) * tq, tq)
        q0 = pl.multiple_of(t_idx * tq, tq)
        cm0, accs = finish(q_prev, carry, q0)
        carry = pair_body(q0, 0, fresh(cm0))
        normalise_store(q_prev, accs)
        rest = t_idx - 1
        carry = lax.fori_loop(
            0, rest // 2, lambda t, c: pair_body(q0, 2 * t + 2, pair_body(q0, 2 * t + 1, c)), carry)
        return lax.fori_loop(t_idx - rest % 2, t_idx, functools.partial(pair_body, q0), carry)

    carry = fresh(tuple(scores(u, 0, sa_ref, 0) for u in nu))
    carry = lax.fori_loop(1, n_q, tile, carry)
    normalise_store((n_q - 1) * tq, finish((n_q - 1) * tq, carry, None)[1])


def _mla_attention(qt, k, vt, w_norm, *, tq):
    b, n_heads, _, s = qt.shape
    kern = functools.partial(_attn_kernel, tq=tq, n_q=s // tq)
    s_scratch = pltpu.VMEM((2, tq // 2, tq), F32)
    vmem = _vmem_limit(
        pipelined=[((2, LANES, s), BF16), ((2, s, LANES), BF16), ((2 * MLA_V, s), BF16), ((s, LANES), BF16)],
        resident=[(s_scratch.shape, F32)] * 2,
        temporaries=4 * (_nbytes((MLA_V + ONES_ROWS, tq), F32) + _nbytes((tq // 2, tq), F32)
                         + _nbytes((tq // 2, tq), BF16)))
    return pl.pallas_call(
        kern,
        grid=(b, n_heads // 2),
        in_specs=[
            pl.BlockSpec((1, 2, LANES, s), lambda i, j: (i, j, 0, 0)),
            pl.BlockSpec((1, 2, s, LANES), lambda i, j: (i, j, 0, 0)),
            pl.BlockSpec((1, 2 * MLA_V, s), lambda i, j: (i, j, 0)),
            pl.BlockSpec((1, LANES), lambda i, j: (0, j)),
        ],
        out_specs=pl.BlockSpec((1, s, LANES), lambda i, j: (i, 0, j)),
        out_shape=jax.ShapeDtypeStruct((b, s, n_heads * MLA_V), BF16),
        scratch_shapes=[s_scratch, s_scratch],
        compiler_params=pltpu.CompilerParams(
            dimension_semantics=("parallel", "parallel"), vmem_limit_bytes=vmem),
        name="mla_attn",
    )(qt, k, vt, w_norm)


def _hgrn_tables(c, g):
    n_lev = int(np.log2(c))
    t = np.arange(c)[:, None]
    s = np.arange(c)[None, :]
    lev = np.full((c, c), n_lev + 1, np.int32)
    for l in range(n_lev):
        m = c >> l
        half = m // 2
        lev[(t // m == s // m) & (t % m >= half) & (s % m < half)] = l
    lev[t == s] = n_lev
    hc = c // 2
    assert (lev[:hc, :hc] == lev[hc:, hc:]).all() and (lev[hc:, :hc] == 0).all()
    tri = np.tril(np.ones((g, g), np.float32))
    return np.concatenate([tri, tri, tri], axis=1), lev[:hc, :hc], n_lev


def _hgrn_level_exponents(b, lf, c):
    sub = lax.broadcasted_iota(jnp.int32, (F32_SUBLANES, LANES), 0)
    out = []
    m = c
    while m >= 2:
        half = m // 2
        if half >= F32_SUBLANES:
            pieces = []
            for lo in range(0, c, m):
                mid = lo + half
                r = b[mid - 1:mid]
                pieces += [r - b[lo:mid], b[mid:mid + half] - r]
            e = jnp.concatenate(pieces, axis=0)
        elif m == 2:
            odd = lax.broadcasted_iota(jnp.int32, lf.shape, 0) % 2 == 1
            e = jnp.where(odd, lf, 0.0)
        else:
            sign = jnp.where(sub % m >= half, 1.0, -1.0)
            pieces = []
            for v0 in range(0, c, F32_SUBLANES):
                ref = b[v0 + half - 1:v0 + half]
                for j in range(1, F32_SUBLANES // m):
                    ref = jnp.where(sub >= j * m, b[v0 + j * m + half - 1:v0 + j * m + half], ref)
                pieces.append(sign * (b[v0:v0 + F32_SUBLANES] - ref))
            e = jnp.concatenate(pieces, axis=0)
        out.append((m, e))
        m = half
    return out


def _hgrn_kernel(q_ref, k_ref, v_ref, lf_ref, g_ref, w_ref, tri_ref, lev_ref, o_ref,
                 *, c, g, n_blocks, n_hd, n_lev):
    tri = tri_ref[...]
    ng = c // g
    hc = c // 2

    def where(i, hd):
        return pl.ds(pl.multiple_of(i * c, c), c), slice(hd * HGRN_HEAD, (hd + 1) * HGRN_HEAD)

    def cumulative_gates(i, hd):
        rows, cols = where(i, hd)
        lf = lf_ref[0, rows, cols]
        lf_w = jnp.concatenate([lf[j * g:(j + 1) * g] for j in range(ng)], axis=1)
        hi = lf_w.astype(BF16)
        r1 = lf_w - hi.astype(F32)
        mid = r1.astype(BF16)
        lo = (r1 - mid.astype(F32)).astype(BF16)
        b_w = _dot(tri, jnp.concatenate([hi, mid, lo], axis=0))
        parts = []
        off = None
        for j in range(ng):
            bj = b_w[:, j * LANES:(j + 1) * LANES]
            if off is not None:
                bj = bj + off
            parts.append(bj)
            off = bj[g - 1:g]
        return lf, jnp.concatenate(parts, axis=0), off

    def intra_weights(i, hd, lf, b, b_end):
        rows, cols = where(i, hd)
        qb = q_ref[0, rows, cols]
        kb = k_ref[0, rows, cols]
        q = qb.astype(F32)
        k = kb.astype(F32)
        lev = lev_ref[...]
        halves = (slice(0, hc), slice(hc, c))
        a_diag = [jnp.zeros((hc, hc), F32), jnp.zeros((hc, hc), F32)]
        a10 = None
        row = lax.broadcasted_iota(jnp.int32, q.shape, 0)
        for l, (m, e) in enumerate(_hgrn_level_exponents(b, lf, c)):
            half = m // 2
            if half >= F32_SUBLANES:
                mixed = jnp.concatenate(
                    [piece for lo in range(0, c, m) for piece in (k[lo:lo + half], q[lo + half:lo + m])],
                    axis=0)
            else:
                mixed = jnp.where(row % m >= half, q, k)
            r = (mixed * jnp.exp2(e)).astype(BF16)
            if l == 0:
                a10 = _dot_nt(r[hc:], r[:hc])
                continue
            for j, rs in enumerate(halves):
                prod = _dot_nt(r[rs], r[rs])
                if half >= F32_SUBLANES:
                    pieces = []
                    for lo in range(0, hc, m):
                        low = slice(lo + half, lo + m)
                        pieces += [a_diag[j][lo:lo + half],
                                   jnp.where(lev[low] == l, prod[low], a_diag[j][low])]
                    a_diag[j] = jnp.concatenate(pieces, axis=0)
                else:
                    a_diag[j] = jnp.where(lev == l, prod, a_diag[j])
        here = lev == n_lev
        for j, rs in enumerate(halves):
            a_diag[j] = jnp.where(here, _dot_nt(qb[rs], kb[rs]), a_diag[j])
        a = jnp.concatenate(
            [jnp.concatenate([a_diag[0], jnp.zeros((hc, hc), F32)], axis=1),
             jnp.concatenate([a10, a_diag[1]], axis=1)], axis=0)
        q_in = (q * jnp.exp2(b)).astype(BF16)
        k_out = (k * jnp.exp2(b_end - b)).astype(BF16)
        return a.astype(BF16), q_in, k_out

    def read_out(i, hd, b_end, a, q_in, k_out, st):
        rows, cols = where(i, hd)
        v = v_ref[0, rows, cols]
        o = _dot_nt(q_in, st.astype(BF16)) + _dot(a, v)
        o = _rms(o, w_ref[:, cols]) * g_ref[0, rows, cols].astype(F32)
        o_ref[0, rows, cols] = o.astype(BF16)
        return st * jnp.exp2(b_end) + _dot_tn(v, k_out)

    init = tuple(jnp.zeros((HGRN_HEAD, HGRN_HEAD), F32) for _ in range(n_hd))

    def trip(t, states):
        units = [(BLOCKS_PER_TRIP * t + j, hd) for j in range(BLOCKS_PER_TRIP) for hd in range(n_hd)]
        gates = [cumulative_gates(i, hd) for i, hd in units]
        states = list(states)
        pending = None
        for u, (i, hd) in enumerate(units):
            lf, b, b_end = gates[u]
            ready = (i, hd, b_end) + intra_weights(i, hd, lf, b, b_end)
            if pending is not None:
                states[pending[1]] = read_out(*pending, states[pending[1]])
            pending = ready
        states[pending[1]] = read_out(*pending, states[pending[1]])
        return tuple(states)

    lax.fori_loop(0, n_blocks // BLOCKS_PER_TRIP, trip, init)


def _hgrn2(hq, kk, hi, lf, g, w_norm, *, heads_per_step):
    b, s, hw = hq.shape
    c = HGRN_BLOCK
    width = heads_per_step * HGRN_HEAD
    tri_np, lev_np, n_lev = _hgrn_tables(c, HGRN_CUMSUM_GROUP)
    tri = jnp.asarray(tri_np, BF16)
    lev = jnp.asarray(lev_np)
    kern = functools.partial(_hgrn_kernel, c=c, g=HGRN_CUMSUM_GROUP, n_blocks=s // c,
                             n_hd=heads_per_step, n_lev=n_lev)
    tok = pl.BlockSpec((1, s, width), lambda i, j: (i, 0, j))
    units = BLOCKS_PER_TRIP * heads_per_step
    vmem = _vmem_limit(
        pipelined=[((5, s, width), BF16), ((s, width), F32), (tri.shape, BF16), (lev.shape, jnp.int32)],
        temporaries=units * (2 * _nbytes((c, HGRN_HEAD), F32) + _nbytes((c, c), F32)
                             + 4 * _nbytes((c, HGRN_HEAD), BF16)))
    return pl.pallas_call(
        kern,
        grid=(b, hw // width),
        in_specs=[tok, tok, tok, tok, tok,
                  pl.BlockSpec((1, width), lambda i, j: (0, j)),
                  pl.BlockSpec(tri.shape, lambda i, j: (0, 0)),
                  pl.BlockSpec(lev.shape, lambda i, j: (0, 0))],
        out_specs=tok,
        out_shape=jax.ShapeDtypeStruct((b, s, hw), BF16),
        compiler_params=pltpu.CompilerParams(
            dimension_semantics=("parallel", "parallel"), vmem_limit_bytes=vmem),
        name="hgrn2",
    )(hq, kk, hi, lf, g, w_norm, tri, lev)


def _out_ffn_kernel(om_ref, oh_ref, x_ref, wo_ref, postw_ref, prew_ref, wg_ref, wu_ref, wd_ref,
                    fpostw_ref, out_ref, *, n_sub):
    sub = x_ref.shape[0] // n_sub
    tiles = [slice(i * sub, (i + 1) * sub) for i in range(n_sub)]
    hs = []
    for rows in tiles:
        mix = jnp.concatenate([om_ref[rows, :], oh_ref[rows, :]], axis=-1)
        hs.append(x_ref[rows, :] + _rms(_dot(mix, wo_ref[...]), postw_ref[...]))
    ffs = []
    for h in hs:
        z = _rms(h, prew_ref[...]).astype(BF16)
        gate = _dot(z, wg_ref[...])
        ffs.append((gate * _sigmoid(gate) * _dot(z, wu_ref[...])).astype(BF16))
    for rows, h, ff in zip(tiles, hs, ffs):
        out_ref[rows, :] = h + _rms(_dot(ff, wd_ref[...]), fpostw_ref[...])


def _out_ffn(om, oh, x, wo, postw, prew, wg, wu, wd, fpostw, *, tm, n_sub):
    n_tok, d = x.shape
    kern = functools.partial(_out_ffn_kernel, n_sub=n_sub)

    def full(a):
        return pl.BlockSpec(a.shape, lambda i: (0,) * a.ndim, pipeline_mode=pl.Buffered(1))

    tok = lambda w: pl.BlockSpec((tm, w), lambda i: (i, 0))
    sub = tm // n_sub
    vmem = _vmem_limit(
        pipelined=[((tm, om.shape[1] + oh.shape[1]), BF16), ((2, tm, d), F32)],
        resident=[(a.shape, a.dtype) for a in (wo, wg, wu, wd)],
        temporaries=2 * (_nbytes((sub, d + 2 * wd.shape[0]), F32) + _nbytes((sub, wd.shape[0]), BF16)))
    return pl.pallas_call(
        kern,
        grid=(n_tok // tm,),
        in_specs=[tok(om.shape[1]), tok(oh.shape[1]), tok(d), full(wo), full(postw), full(prew),
                  full(wg), full(wu), full(wd), full(fpostw)],
        out_specs=tok(d),
        out_shape=jax.ShapeDtypeStruct((n_tok, d), F32),
        compiler_params=pltpu.CompilerParams(
            dimension_semantics=("parallel",), vmem_limit_bytes=vmem),
        name="out_ffn",
    )(om, oh, x, wo, postw, prew, wg, wu, wd, fpostw)


def _prep_in_proj_weights(w_in, w_uq, w_ukv, *, q_rank, kv_rank):
    d = w_in.shape[0]
    half = MLA_ROPE // 2
    s2 = q_rank + kv_rank
    s3 = s2 + MLA_ROPE
    kr = w_in[:, s2:s3]
    z_lo = jnp.zeros((d, MLA_NOPE), w_in.dtype)
    kr_slab = jnp.concatenate([z_lo, kr, kr[:, half:], kr[:, :half]], axis=1)
    win_ext = jnp.concatenate([w_in[:, :s2], kr_slab, w_in[:, s3:]], axis=1).astype(BF16)

    n_heads = w_uq.shape[1]
    rope = w_uq[:, :, MLA_NOPE:]
    wq_pad = jnp.concatenate([w_uq, rope[:, :, half:], rope[:, :, :half]], axis=2)
    wq_pad = wq_pad.reshape(q_rank, n_heads * LANES).astype(BF16)

    wk_pad = jnp.pad(w_ukv[:, :, :MLA_NOPE], ((0, 0), (0, 0), (0, LANES - MLA_NOPE)))
    wv = w_ukv[:, :, MLA_NOPE:]
    wkv_cat = jnp.concatenate([wk_pad.reshape(kv_rank, n_heads * LANES),
                               wv.reshape(kv_rank, n_heads * MLA_V)], axis=1).astype(BF16)
    return win_ext, wq_pad, wkv_cat


def kernel(x, positions, attn_pre_norm, w_in, mla_q_norm, mla_w_uq, mla_kv_norm, mla_w_ukv, mla_out_norm,
           hgrn_lb_logits, hgrn_out_norm, w_out, attn_post_norm, ffn_pre_norm, w_gate, w_up, w_down,
           ffn_post_norm):
    b, s, d = x.shape
    assert attn_pre_norm.shape[0] == 1, "single-layer block"
    assert mla_w_uq.shape[3] == MLA_NOPE + MLA_ROPE and 2 * MLA_ROPE + MLA_NOPE == LANES
    q_rank = mla_q_norm.shape[-1]
    kv_rank = mla_kv_norm.shape[-1]
    n_heads = mla_w_uq.shape[2]
    hw = hgrn_out_norm.shape[-1]
    row = lambda a: a.reshape(1, -1)

    win_ext, wq_pad, wkv_cat = _prep_in_proj_weights(w_in[0], mla_w_uq[0], mla_w_ukv[0],
                                                     q_rank=q_rank, kv_rank=kv_rank)
    qt, k, vt, hq, lf, kk, hi, g = _in_proj(
        x, positions, row(attn_pre_norm[0]), win_ext, row(mla_q_norm[0]), wq_pad,
        row(mla_kv_norm[0]), wkv_cat, hgrn_lb_logits, n_heads=n_heads, hw=hw, tm=1024, n_sub=4)
    o_mla = _mla_attention(qt, k, vt, row(mla_out_norm[0]), tq=512)
    o_hgrn = _hgrn2(hq, kk, hi, lf, g, row(hgrn_out_norm[0]), heads_per_step=2)

    out = _out_ffn(o_mla.reshape(b * s, -1), o_hgrn.reshape(b * s, -1), x.reshape(b * s, d),
                   w_out[0].astype(BF16), row(attn_post_norm[0]), row(ffn_pre_norm[0]),
                   w_gate[0].astype(BF16), w_up[0].astype(BF16), w_down[0].astype(BF16),
                   row(ffn_post_norm[0]), tm=1024, n_sub=4)
    return out.reshape(b, s, d)
```

```python
import functools
import math

import numpy as np
import jax
import jax.numpy as jnp
from jax import lax
from jax.experimental import pallas as pl
from jax.experimental.pallas import tpu as pltpu

F32 = jnp.float32
BF16 = jnp.bfloat16

EPS = 1e-6
ROPE_THETA = 10000.0
LOG2E = math.log2(math.e)
LANES = 128
BF16_SUBLANES = 16
MLA_NOPE = 64
MLA_ROPE = 32
MLA_V = 64
ONES_ROWS = BF16_SUBLANES
F32_SUBLANES = 8
HGRN_HEAD = 128
HGRN_BLOCK = 128
HGRN_CUMSUM_GROUP = 64
BLOCKS_PER_TRIP = 8
NEG_BIG = -1e30
MIB = 1024 * 1024
V7X_VMEM_BYTES = 64 * MIB
MIN_VMEM_REQUEST = 3 * V7X_VMEM_BYTES // 4


def _nbytes(shape, dtype):
    return math.prod(shape) * jnp.dtype(dtype).itemsize


def _vmem_limit(pipelined, resident=(), temporaries=0):
    total = 2 * sum(_nbytes(s, d) for s, d in pipelined) + sum(_nbytes(s, d) for s, d in resident)
    return min(max(total + temporaries, MIN_VMEM_REQUEST), V7X_VMEM_BYTES)


def _rms(x, w):
    inv = lax.rsqrt(jnp.mean(x * x, axis=-1, keepdims=True) + EPS)
    return (x * inv) * w


def _sigmoid(x):
    return 1.0 / (1.0 + jnp.exp(-x))


def _dot(a, b):
    return jnp.dot(a, b, preferred_element_type=F32)


def _dot_nt(a, b):
    return lax.dot_general(a, b, (((1,), (1,)), ((), ())), preferred_element_type=F32)


def _dot_tn(a, b):
    return lax.dot_general(a, b, (((0,), (0,)), ((), ())), preferred_element_type=F32)


def _rope_kernel(pos_ref, invf_ref, cos_ref, sin_ref):
    ang = pos_ref[...].astype(F32) * invf_ref[...]
    cos_ref[...] = jnp.cos(ang)
    sin_ref[...] = jnp.sin(ang)


def _rope_tables(positions):
    n_tok = positions.size
    half = MLA_ROPE // 2
    inv_freq = 1.0 / (ROPE_THETA ** (jnp.arange(0, MLA_ROPE, 2, dtype=F32) / MLA_ROPE))
    blk = min(n_tok, 4096)
    spec = pl.BlockSpec((half, blk), lambda i: (0, i))
    shp = jax.ShapeDtypeStruct((half, n_tok), F32)
    return pl.pallas_call(
        _rope_kernel,
        grid=(n_tok // blk,),
        in_specs=[pl.BlockSpec((1, blk), lambda i: (0, i)), pl.BlockSpec((half, 1), lambda i: (0, 0))],
        out_specs=[spec, spec],
        out_shape=[shp, shp],
        name="rope_tables",
    )(positions.reshape(1, n_tok), inv_freq.reshape(half, 1))


def _inproj_kernel(x_ref, ct_ref, st_ref, prew_ref, win_ref, qnw_ref, wq_ref,
                   kvnw_ref, wkv_ref, lbl_ref,
                   qt_out, k_out, vt_out, hq_out, lf_out, kk_out, hi_out, g_out,
                   *, qscale, n_heads, q_rank, kv_rank, hw, n_sub):
    sub = x_ref.shape[1] // n_sub
    tiles = [slice(i * sub, (i + 1) * sub) for i in range(n_sub)]
    r0, r1 = MLA_NOPE, MLA_NOPE + MLA_ROPE
    hp = n_heads * LANES
    c0 = q_rank + kv_rank + LANES

    lbl = lbl_ref[...]
    e = jnp.exp(lbl - jnp.max(lbl, axis=0, keepdims=True))
    lb = e[0:1] / jnp.sum(e, axis=0, keepdims=True)

    us = [_rms(x_ref[0, rows, :], prew_ref[...]).astype(BF16) for rows in tiles]
    lows = [_dot(u, win_ref[:, :c0]) for u in us]

    def gate_group(g, u, rows):
        y = _dot(u, win_ref[:, c0 + g * hw:c0 + (g + 1) * hw])
        if g == 0:
            hq_out[0, rows, :] = (y * _sigmoid(y)).astype(BF16)
        elif g == 1:
            f = lb + (1.0 - lb) * _sigmoid(y)
            lf_out[0, rows, :] = jnp.log(f) * LOG2E
            kk_out[0, rows, :] = (1.0 - f).astype(BF16)
        elif g == 2:
            hi_out[0, rows, :] = y.astype(BF16)
        else:
            g_out[0, rows, :] = (y * _sigmoid(y)).astype(BF16)

    def mla_group(low, rows):
        c_q = low[:, :q_rank]
        c_kv = low[:, q_rank:q_rank + kv_rank]
        kr = low[:, q_rank + kv_rank:]
        ct = ct_ref[:, rows]
        st = st_ref[:, rows]
        cc = jnp.concatenate([ct, ct], axis=0)
        ss = jnp.concatenate([-st, st], axis=0)

        cqn = _rms(c_q, qnw_ref[...]).astype(BF16)
        qq = _dot(cqn, wq_ref[...])
        ccq = cc * qscale
        ssq = ss * qscale
        zpad = jnp.zeros((LANES - r1, sub), F32)
        for h in range(n_heads):
            xt = qq[:, h * LANES:(h + 1) * LANES].T
            rope = xt[r0:r1] * ccq + xt[r1:] * ssq
            qt_out[0, h, :, rows] = jnp.concatenate([xt[:r0] * qscale, rope, zpad], axis=0).astype(BF16)

        z64 = jnp.zeros((r0, sub), F32)
        ck = jnp.concatenate([z64, cc, zpad], axis=0).T
        sk = jnp.concatenate([z64, zpad, ss], axis=0).T
        k_rope = kr * ck + pltpu.roll(kr * sk, LANES - MLA_ROPE, 1)

        ckvn = _rms(c_kv, kvnw_ref[...]).astype(BF16)
        kvv = _dot(ckvn, wkv_ref[...])
        for h in range(n_heads):
            k_out[0, h, rows, :] = (kvv[:, h * LANES:(h + 1) * LANES] + k_rope).astype(BF16)
        vt_out[0, :, rows] = kvv[:, hp:].T.astype(BF16)

    for u, low, rows in zip(us, lows, tiles):
        gate_group(0, u, rows)
        mla_group(low, rows)
    for g in (1, 2, 3):
        for u, rows in zip(us, tiles):
            gate_group(g, u, rows)


def _in_proj(x, cos_t, sin_t, prew, win_ext, qnw, wq_pad, kvnw, wkv_cat, lbl, *, n_heads, hw, tm, n_sub):
    b, s, d = x.shape
    q_rank = qnw.shape[-1]
    kv_rank = kvnw.shape[-1]
    ns = s // tm
    qscale = (MLA_NOPE + MLA_ROPE) ** -0.5 * LOG2E
    kern = functools.partial(_inproj_kernel, qscale=qscale, n_heads=n_heads, q_rank=q_rank,
                             kv_rank=kv_rank, hw=hw, n_sub=n_sub)

    def full(a):
        return pl.BlockSpec(a.shape, lambda i, j: (0,) * a.ndim)

    tok = lambda w: pl.BlockSpec((1, tm, w), lambda i, j: (i, j, 0))
    head = pl.BlockSpec((1, n_heads, tm, LANES), lambda i, j: (i, 0, j, 0))
    head_t = pl.BlockSpec((1, n_heads, LANES, tm), lambda i, j: (i, 0, 0, j))
    tab_t = pl.BlockSpec((cos_t.shape[0], tm), lambda i, j: (0, i * ns + j))
    vw = n_heads * MLA_V
    out_shape = [
        jax.ShapeDtypeStruct((b, n_heads, LANES, s), BF16),
        jax.ShapeDtypeStruct((b, n_heads, s, LANES), BF16),
        jax.ShapeDtypeStruct((b, vw, s), BF16),
        jax.ShapeDtypeStruct((b, s, hw), BF16),
        jax.ShapeDtypeStruct((b, s, hw), F32),
        jax.ShapeDtypeStruct((b, s, hw), BF16),
        jax.ShapeDtypeStruct((b, s, hw), BF16),
        jax.ShapeDtypeStruct((b, s, hw), BF16),
    ]
    sub = tm // n_sub
    vmem = _vmem_limit(
        pipelined=[((tm, d), F32), ((2 * cos_t.shape[0], tm), F32), ((2, n_heads, tm, LANES), BF16),
                   ((vw, tm), BF16), ((tm, hw), F32), ((4, tm, hw), BF16)]
        + [(a.shape, a.dtype) for a in (win_ext, wq_pad, wkv_cat)],
        temporaries=2 * _nbytes((sub, win_ext.shape[1] - 3 * hw + wq_pad.shape[1] + wkv_cat.shape[1] + hw), F32))
    return pl.pallas_call(
        kern,
        grid=(b, ns),
        in_specs=[tok(d), tab_t, tab_t, full(prew), full(win_ext), full(qnw), full(wq_pad),
                  full(kvnw), full(wkv_cat), full(lbl)],
        out_specs=[head_t, head, pl.BlockSpec((1, vw, tm), lambda i, j: (i, 0, j)),
                   tok(hw), tok(hw), tok(hw), tok(hw), tok(hw)],
        out_shape=out_shape,
        compiler_params=pltpu.CompilerParams(
            dimension_semantics=("parallel", "parallel"), vmem_limit_bytes=vmem),
        name="in_proj",
    )(x, cos_t, sin_t, prew, win_ext, qnw, wq_pad, kvnw, wkv_cat, lbl)


def _attn_kernel(qt_ref, k_ref, vt_ref, w_ref, o_ref, sa_ref, sb_ref, *, tq, n_q):
    kc = tq // 2
    ones = jnp.ones((ONES_ROWS, kc), BF16)
    nu = range(2)

    def scores(u, ks, s_ref, q_start):
        s_t = _dot(k_ref[0, u, pl.ds(ks, kc), :], qt_ref[0, u, :, pl.ds(q_start, tq)])
        s_ref[u] = s_t
        return jnp.max(s_t, axis=0, keepdims=True)

    def scores_right_half(u, ks, s_ref, q_start):
        s_ref[u, :, kc:] = _dot(k_ref[0, u, pl.ds(ks, kc), :], qt_ref[0, u, :, pl.ds(q_start + kc, kc)])

    def values(u, ks):
        return jnp.concatenate([vt_ref[0, u * MLA_V:(u + 1) * MLA_V, pl.ds(ks, kc)], ones], axis=0)

    def causal(s_sq):
        key = lax.broadcasted_iota(jnp.int32, s_sq.shape, 0)
        qry = lax.broadcasted_iota(jnp.int32, s_sq.shape, 1)
        return jnp.where(key <= qry, s_sq, -jnp.inf)

    def update(s_t, cmax, m, acc, v_ext):
        m_new = jnp.maximum(m, cmax)
        p_t = jnp.exp2(s_t - m_new).astype(BF16)
        return m_new, jnp.exp2(m - m_new) * acc + _dot(v_ext, p_t)

    def accumulate(u, ks, s_ref, cmax, m, acc):
        return update(s_ref[u], cmax, m, acc, values(u, ks))

    def accumulate_first_diagonal(u, q0, s_ref, cmax, m, acc):
        s_t = s_ref[u]
        left = causal(s_t[:, :kc])
        s_t = jnp.concatenate([left, s_t[:, kc:]], axis=1)
        cmax = jnp.concatenate([jnp.max(left, axis=0, keepdims=True), cmax[:, kc:]], axis=1)
        return update(s_t, cmax, m, acc, values(u, q0))

    def accumulate_second_diagonal(u, q0, s_ref, m, acc):
        s_t = causal(s_ref[u, :, kc:])
        m_r, acc_r = update(s_t, jnp.max(s_t, axis=0, keepdims=True), m[:, kc:], acc[:, kc:],
                            values(u, q0 + kc))
        return (jnp.concatenate([m[:, :kc], m_r], axis=1),
                jnp.concatenate([acc[:, :kc], acc_r], axis=1))

    def pair_body(q0, t, carry):
        k0 = t * tq if isinstance(t, int) else pl.multiple_of(t * tq, tq)
        cm_b = [scores(u, k0 + kc, sb_ref, q0) for u in nu]
        st = [accumulate(u, k0, sa_ref, carry[u][0], carry[u][1], carry[u][2]) for u in nu]
        cm_a = [scores(u, k0 + tq, sa_ref, q0) for u in nu]
        st = [accumulate(u, k0 + kc, sb_ref, cm_b[u], st[u][0], st[u][1]) for u in nu]
        return tuple((cm_a[u], st[u][0], st[u][1]) for u in nu)

    def fresh(cm_first):
        return tuple((cm_first[u], jnp.full((1, tq), NEG_BIG, F32),
                      jnp.zeros((MLA_V + ONES_ROWS, tq), F32)) for u in nu)

    def finish(q0, carry, q_next):
        for u in nu:
            scores_right_half(u, q0 + kc, sb_ref, q0)
        st = [accumulate_first_diagonal(u, q0, sa_ref, carry[u][0], carry[u][1], carry[u][2]) for u in nu]
        cm_next = [] if q_next is None else [scores(u, 0, sa_ref, q_next) for u in nu[:-1]]
        st = [accumulate_second_diagonal(u, q0, sb_ref, st[u][0], st[u][1]) for u in nu]
        if q_next is not None:
            cm_next.append(scores(nu[-1], 0, sa_ref, q_next))
        return tuple(cm_next), [st[u][1] for u in nu]

    def normalise_store(q0, accs):
        outs = []
        for acc in accs:
            o = acc[:MLA_V] / acc[MLA_V:MLA_V + 1]
            outs.append(o * lax.rsqrt(jnp.mean(o * o, axis=0, keepdims=True) + EPS))
        pair = jnp.concatenate(outs, axis=0).T
        o_ref[0, pl.ds(q0, tq), :] = (pair * w_ref[...]).astype(BF16)

    def tile(t_idx, carry):
        q_prev = pl.multiple_of((t_idx - 1) * tq, tq)
        q0 = pl.multiple_of(t_idx * tq, tq)
        cm0, accs = finish(q_prev, carry, q0)
        carry = pair_body(q0, 0, fresh(cm0))
        normalise_store(q_prev, accs)
        rest = t_idx - 1
        carry = lax.fori_loop(
            0, rest // 2, lambda t, c: pair_body(q0, 2 * t + 2, pair_body(q0, 2 * t + 1, c)), carry)
        return lax.fori_loop(t_idx - rest % 2, t_idx, functools.partial(pair_body, q0), carry)

    carry = fresh(tuple(scores(u, 0, sa_ref, 0) for u in nu))
    carry = lax.fori_loop(1, n_q, tile, carry)
    normalise_store((n_q - 1) * tq, finish((n_q - 1) * tq, carry, None)[1])


def _mla_attention(qt, k, vt, w_norm, *, tq):
    b, n_heads, _, s = qt.shape
    kern = functools.partial(_attn_kernel, tq=tq, n_q=s // tq)
    s_scratch = pltpu.VMEM((2, tq // 2, tq), F32)
    vmem = _vmem_limit(
        pipelined=[((2, LANES, s), BF16), ((2, s, LANES), BF16), ((2 * MLA_V, s), BF16), ((s, LANES), BF16)],
        resident=[(s_scratch.shape, F32)] * 2,
        temporaries=4 * (_nbytes((MLA_V + ONES_ROWS, tq), F32) + _nbytes((tq // 2, tq), F32)
                         + _nbytes((tq // 2, tq), BF16)))
    return pl.pallas_call(
        kern,
        grid=(b, n_heads // 2),
        in_specs=[
            pl.BlockSpec((1, 2, LANES, s), lambda i, j: (i, j, 0, 0)),
            pl.BlockSpec((1, 2, s, LANES), lambda i, j: (i, j, 0, 0)),
            pl.BlockSpec((1, 2 * MLA_V, s), lambda i, j: (i, j, 0)),
            pl.BlockSpec((1, LANES), lambda i, j: (0, j)),
        ],
        out_specs=pl.BlockSpec((1, s, LANES), lambda i, j: (i, 0, j)),
        out_shape=jax.ShapeDtypeStruct((b, s, n_heads * MLA_V), BF16),
        scratch_shapes=[s_scratch, s_scratch],
        compiler_params=pltpu.CompilerParams(
            dimension_semantics=("parallel", "parallel"), vmem_limit_bytes=vmem),
        name="mla_attn",
    )(qt, k, vt, w_norm)


def _hgrn_tables(c, g):
    n_lev = int(np.log2(c))
    t = np.arange(c)[:, None]
    s = np.arange(c)[None, :]
    lev = np.full((c, c), n_lev + 1, np.int32)
    for l in range(n_lev):
        m = c >> l
        half = m // 2
        lev[(t // m == s // m) & (t % m >= half) & (s % m < half)] = l
    lev[t == s] = n_lev
    hc = c // 2
    assert (lev[:hc, :hc] == lev[hc:, hc:]).all() and (lev[hc:, :hc] == 0).all()
    tri = np.tril(np.ones((g, g), np.float32))
    return np.concatenate([tri, tri, tri], axis=1), lev[:hc, :hc], n_lev


def _hgrn_level_exponents(b, lf, c):
    sub = lax.broadcasted_iota(jnp.int32, (F32_SUBLANES, LANES), 0)
    out = []
    m = c
    while m >= 2:
        half = m // 2
        if half >= F32_SUBLANES:
            pieces = []
            for lo in range(0, c, m):
                mid = lo + half
                r = b[mid - 1:mid]
                pieces += [r - b[lo:mid], b[mid:mid + half] - r]
            e = jnp.concatenate(pieces, axis=0)
        elif m == 2:
            odd = lax.broadcasted_iota(jnp.int32, lf.shape, 0) % 2 == 1
            e = jnp.where(odd, lf, 0.0)
        else:
            sign = jnp.where(sub % m >= half, 1.0, -1.0)
            pieces = []
            for v0 in range(0, c, F32_SUBLANES):
                ref = b[v0 + half - 1:v0 + half]
                for j in range(1, F32_SUBLANES // m):
                    ref = jnp.where(sub >= j * m, b[v0 + j * m + half - 1:v0 + j * m + half], ref)
                pieces.append(sign * (b[v0:v0 + F32_SUBLANES] - ref))
            e = jnp.concatenate(pieces, axis=0)
        out.append((m, e))
        m = half
    return out


def _hgrn_kernel(q_ref, k_ref, v_ref, lf_ref, g_ref, w_ref, tri_ref, lev_ref, o_ref,
                 *, c, g, n_blocks, n_hd, n_lev):
    tri = tri_ref[...]
    ng = c // g
    hc = c // 2

    def where(i, hd):
        return pl.ds(pl.multiple_of(i * c, c), c), slice(hd * HGRN_HEAD, (hd + 1) * HGRN_HEAD)

    def cumulative_gates(i, hd):
        rows, cols = where(i, hd)
        lf = lf_ref[0, rows, cols]
        lf_w = jnp.concatenate([lf[j * g:(j + 1) * g] for j in range(ng)], axis=1)
        hi = lf_w.astype(BF16)
        r1 = lf_w - hi.astype(F32)
        mid = r1.astype(BF16)
        lo = (r1 - mid.astype(F32)).astype(BF16)
        b_w = _dot(tri, jnp.concatenate([hi, mid, lo], axis=0))
        parts = []
        off = None
        for j in range(ng):
            bj = b_w[:, j * LANES:(j + 1) * LANES]
            if off is not None:
                bj = bj + off
            parts.append(bj)
            off = bj[g - 1:g]
        return lf, jnp.concatenate(parts, axis=0), off

    def intra_weights(i, hd, lf, b, b_end):
        rows, cols = where(i, hd)
        qb = q_ref[0, rows, cols]
        kb = k_ref[0, rows, cols]
        q = qb.astype(F32)
        k = kb.astype(F32)
        lev = lev_ref[...]
        halves = (slice(0, hc), slice(hc, c))
        a_diag = [jnp.zeros((hc, hc), F32), jnp.zeros((hc, hc), F32)]
        a10 = None
        row = lax.broadcasted_iota(jnp.int32, q.shape, 0)
        for l, (m, e) in enumerate(_hgrn_level_exponents(b, lf, c)):
            half = m // 2
            if half >= F32_SUBLANES:
                mixed = jnp.concatenate(
                    [piece for lo in range(0, c, m) for piece in (k[lo:lo + half], q[lo + half:lo + m])],
                    axis=0)
            else:
                mixed = jnp.where(row % m >= half, q, k)
            r = (mixed * jnp.exp2(e)).astype(BF16)
            if l == 0:
                a10 = _dot_nt(r[hc:], r[:hc])
                continue
            for j, rs in enumerate(halves):
                prod = _dot_nt(r[rs], r[rs])
                if half >= F32_SUBLANES:
                    pieces = []
                    for lo in range(0, hc, m):
                        low = slice(lo + half, lo + m)
                        pieces += [a_diag[j][lo:lo + half],
                                   jnp.where(lev[low] == l, prod[low], a_diag[j][low])]
                    a_diag[j] = jnp.concatenate(pieces, axis=0)
                else:
                    a_diag[j] = jnp.where(lev == l, prod, a_diag[j])
        here = lev == n_lev
        for j, rs in enumerate(halves):
            a_diag[j] = jnp.where(here, _dot_nt(qb[rs], kb[rs]), a_diag[j])
        a = jnp.concatenate(
            [jnp.concatenate([a_diag[0], jnp.zeros((hc, hc), F32)], axis=1),
             jnp.concatenate([a10, a_diag[1]], axis=1)], axis=0)
        q_in = (q * jnp.exp2(b)).astype(BF16)
        k_out = (k * jnp.exp2(b_end - b)).astype(BF16)
        return a.astype(BF16), q_in, k_out

    def read_out(i, hd, b_end, a, q_in, k_out, st):
        rows, cols = where(i, hd)
        v = v_ref[0, rows, cols]
        o = _dot_nt(q_in, st.astype(BF16)) + _dot(a, v)
        o = _rms(o, w_ref[:, cols]) * g_ref[0, rows, cols].astype(F32)
        o_ref[0, rows, cols] = o.astype(BF16)
        return st * jnp.exp2(b_end) + _dot_tn(v, k_out)

    init = tuple(jnp.zeros((HGRN_HEAD, HGRN_HEAD), F32) for _ in range(n_hd))

    def trip(t, states):
        units = [(BLOCKS_PER_TRIP * t + j, hd) for j in range(BLOCKS_PER_TRIP) for hd in range(n_hd)]
        gates = [cumulative_gates(i, hd) for i, hd in units]
        states = list(states)
        pending = None
        for u, (i, hd) in enumerate(units):
            lf, b, b_end = gates[u]
            ready = (i, hd, b_end) + intra_weights(i, hd, lf, b, b_end)
            if pending is not None:
                states[pending[1]] = read_out(*pending, states[pending[1]])
            pending = ready
        states[pending[1]] = read_out(*pending, states[pending[1]])
        return tuple(states)

    lax.fori_loop(0, n_blocks // BLOCKS_PER_TRIP, trip, init)


def _hgrn2(hq, kk, hi, lf, g, w_norm, *, heads_per_step):
    b, s, hw = hq.shape
    c = HGRN_BLOCK
    width = heads_per_step * HGRN_HEAD
    tri_np, lev_np, n_lev = _hgrn_tables(c, HGRN_CUMSUM_GROUP)
    tri = jnp.asarray(tri_np, BF16)
    lev = jnp.asarray(lev_np)
    kern = functools.partial(_hgrn_kernel, c=c, g=HGRN_CUMSUM_GROUP, n_blocks=s // c,
                             n_hd=heads_per_step, n_lev=n_lev)
    tok = pl.BlockSpec((1, s, width), lambda i, j: (i, 0, j))
    units = BLOCKS_PER_TRIP * heads_per_step
    vmem = _vmem_limit(
        pipelined=[((5, s, width), BF16), ((s, width), F32), (tri.shape, BF16), (lev.shape, jnp.int32)],
        temporaries=units * (2 * _nbytes((c, HGRN_HEAD), F32) + _nbytes((c, c), F32)
                             + 4 * _nbytes((c, HGRN_HEAD), BF16)))
    return pl.pallas_call(
        kern,
        grid=(b, hw // width),
        in_specs=[tok, tok, tok, tok, tok,
                  pl.BlockSpec((1, width), lambda i, j: (0, j)),
                  pl.BlockSpec(tri.shape, lambda i, j: (0, 0)),
                  pl.BlockSpec(lev.shape, lambda i, j: (0, 0))],
        out_specs=tok,
        out_shape=jax.ShapeDtypeStruct((b, s, hw), BF16),
        compiler_params=pltpu.CompilerParams(
            dimension_semantics=("parallel", "parallel"), vmem_limit_bytes=vmem),
        name="hgrn2",
    )(hq, kk, hi, lf, g, w_norm, tri, lev)


def _out_ffn_kernel(om_ref, oh_ref, x_ref, wo_ref, postw_ref, prew_ref, wg_ref, wu_ref, wd_ref,
                    fpostw_ref, out_ref, *, n_sub):
    sub = x_ref.shape[0] // n_sub
    tiles = [slice(i * sub, (i + 1) * sub) for i in range(n_sub)]
    hs = []
    for rows in tiles:
        mix = jnp.concatenate([om_ref[rows, :], oh_ref[rows, :]], axis=-1)
        hs.append(x_ref[rows, :] + _rms(_dot(mix, wo_ref[...]), postw_ref[...]))
    ffs = []
    for h in hs:
        z = _rms(h, prew_ref[...]).astype(BF16)
        gate = _dot(z, wg_ref[...])
        ffs.append((gate * _sigmoid(gate) * _dot(z, wu_ref[...])).astype(BF16))
    for rows, h, ff in zip(tiles, hs, ffs):
        out_ref[rows, :] = h + _rms(_dot(ff, wd_ref[...]), fpostw_ref[...])


def _out_ffn(om, oh, x, wo, postw, prew, wg, wu, wd, fpostw, *, tm, n_sub):
    n_tok, d = x.shape
    kern = functools.partial(_out_ffn_kernel, n_sub=n_sub)

    def full(a):
        return pl.BlockSpec(a.shape, lambda i: (0,) * a.ndim, pipeline_mode=pl.Buffered(1))

    tok = lambda w: pl.BlockSpec((tm, w), lambda i: (i, 0))
    sub = tm // n_sub
    vmem = _vmem_limit(
        pipelined=[((tm, om.shape[1] + oh.shape[1]), BF16), ((2, tm, d), F32)],
        resident=[(a.shape, a.dtype) for a in (wo, wg, wu, wd)],
        temporaries=2 * (_nbytes((sub, d + 2 * wd.shape[0]), F32) + _nbytes((sub, wd.shape[0]), BF16)))
    return pl.pallas_call(
        kern,
        grid=(n_tok // tm,),
        in_specs=[tok(om.shape[1]), tok(oh.shape[1]), tok(d), full(wo), full(postw), full(prew),
                  full(wg), full(wu), full(wd), full(fpostw)],
        out_specs=tok(d),
        out_shape=jax.ShapeDtypeStruct((n_tok, d), F32),
        compiler_params=pltpu.CompilerParams(
            dimension_semantics=("parallel",), vmem_limit_bytes=vmem),
        name="out_ffn",
    )(om, oh, x, wo, postw, prew, wg, wu, wd, fpostw)


def _prep_in_proj_weights(w_in, w_uq, w_ukv, *, q_rank, kv_rank):
    d = w_in.shape[0]
    half = MLA_ROPE // 2
    s2 = q_rank + kv_rank
    s3 = s2 + MLA_ROPE
    kr = w_in[:, s2:s3]
    z_lo = jnp.zeros((d, MLA_NOPE), w_in.dtype)
    kr_slab = jnp.concatenate([z_lo, kr, kr[:, half:], kr[:, :half]], axis=1)
    win_ext = jnp.concatenate([w_in[:, :s2], kr_slab, w_in[:, s3:]], axis=1).astype(BF16)

    n_heads = w_uq.shape[1]
    rope = w_uq[:, :, MLA_NOPE:]
    wq_pad = jnp.concatenate([w_uq, rope[:, :, half:], rope[:, :, :half]], axis=2)
    wq_pad = wq_pad.reshape(q_rank, n_heads * LANES).astype(BF16)

    wk_pad = jnp.pad(w_ukv[:, :, :MLA_NOPE], ((0, 0), (0, 0), (0, LANES - MLA_NOPE)))
    wv = w_ukv[:, :, MLA_NOPE:]
    wkv_cat = jnp.concatenate([wk_pad.reshape(kv_rank, n_heads * LANES),
                               wv.reshape(kv_rank, n_heads * MLA_V)], axis=1).astype(BF16)
    return win_ext, wq_pad, wkv_cat


def kernel(x, positions, attn_pre_norm, w_in, mla_q_norm, mla_w_uq, mla_kv_norm, mla_w_ukv, mla_out_norm,
           hgrn_lb_logits, hgrn_out_norm, w_out, attn_post_norm, ffn_pre_norm, w_gate, w_up, w_down,
           ffn_post_norm):
    b, s, d = x.shape
    assert attn_pre_norm.shape[0] == 1, "single-layer block"
    assert mla_w_uq.shape[3] == MLA_NOPE + MLA_ROPE and 2 * MLA_ROPE + MLA_NOPE == LANES
    q_rank = mla_q_norm.shape[-1]
    kv_rank = mla_kv_norm.shape[-1]
    n_heads = mla_w_uq.shape[2]
    hw = hgrn_out_norm.shape[-1]
    row = lambda a: a.reshape(1, -1)

    cos_t, sin_t = _rope_tables(positions)
    win_ext, wq_pad, wkv_cat = _prep_in_proj_weights(w_in[0], mla_w_uq[0], mla_w_ukv[0],
                                                     q_rank=q_rank, kv_rank=kv_rank)
    qt, k, vt, hq, lf, kk, hi, g = _in_proj(
        x, cos_t, sin_t, row(attn_pre_norm[0]), win_ext, row(mla_q_norm[0]), wq_pad,
        row(mla_kv_norm[0]), wkv_cat, hgrn_lb_logits, n_heads=n_heads, hw=hw, tm=1024, n_sub=4)
    o_mla = _mla_attention(qt, k, vt, row(mla_out_norm[0]), tq=512)
    o_hgrn = _hgrn2(hq, kk, hi, lf, g, row(hgrn_out_norm[0]), heads_per_step=2)

    out = _out_ffn(o_mla.reshape(b * s, -1), o_hgrn.reshape(b * s, -1), x.reshape(b * s, d),
                   w_out[0].astype(BF16), row(attn_post_norm[0]), row(ffn_pre_norm[0]),
                   w_gate[0].astype(BF16), w_up[0].astype(BF16), w_down[0].astype(BF16),
                   row(ffn_post_norm[0]), tm=1024, n_sub=4)
    return out.reshape(b, s, d)
```
